```python
import jax, jax.numpy as jnp
from jax import lax
import numpy as np

D_MODEL = 1024
BATCH = 8
SEQ = 2048
DEPTH = 4
DEC_BATCH = 16
DEC_SEQ = 4096
PAST_LEN = 128

MIX_WIDTH = D_MODEL
D_CONV = MIX_WIDTH // 2
D_FFT = MIX_WIDTH - D_CONV
HEAD_DIM = 64
N_CONV_HEADS = D_CONV // HEAD_DIM
N_FFT_GROUPS = D_FFT // HEAD_DIM
CONV_WIDTH = 3
IN_WIDTH = 3 * D_CONV + D_FFT
D_FF = ((8 * D_MODEL // 3 + 255) // 256) * 256
N_MOD = 6
EPS = 1e-6

kernel_name = "hybrid_conv_fourier_encoder"


def rms_norm(x, g):
    xf = x.astype(jnp.float32)
    y = xf * lax.rsqrt(jnp.mean(xf * xf, axis=-1, keepdims=True) + EPS)
    return (y * g.astype(jnp.float32)).astype(x.dtype)


def centred_depthwise_conv3(u, w, b):
    up = jnp.pad(u, ((0, 0), (1, 1), (0, 0)))
    return up[:, :-2] * w[0] + up[:, 1:-1] * w[1] + up[:, 2:] * w[2] + b


def fourier_groups(f):
    bsz, seq, _ = f.shape
    fg = f.reshape(bsz, seq, N_FFT_GROUPS, HEAD_DIM).astype(jnp.float32)
    out = jnp.fft.fft2(fg, axes=(1, 3)).real
    return out.reshape(bsz, seq, D_FFT).astype(f.dtype)


def run_trunk(x, c, w_ada, b_ada, g_pre_mix, g_post_mix, w_in, conv_w, conv_b,
              g_conv, g_fft, w_out, g_pre_ffn, g_post_ffn, w_gate, w_up, w_down):
    c_act = jax.nn.silu(c)
    for l in range(DEPTH):
        mod = c_act @ w_ada[l] + b_ada[l]
        sh_m, sc_m, gt_m, sh_f, sc_f, gt_f = [m[:, None, :] for m in jnp.split(mod, N_MOD, axis=-1)]

        h = rms_norm(x, g_pre_mix[l]) * (1.0 + sc_m) + sh_m
        z = h @ w_in[l]
        bg = z[..., :D_CONV]
        cg = z[..., D_CONV:2 * D_CONV]
        v = z[..., 2 * D_CONV:3 * D_CONV]
        f = z[..., 3 * D_CONV:]
        conv_out = bg * centred_depthwise_conv3(cg * v, conv_w[l], conv_b[l])
        fft_out = fourier_groups(f)
        merged = jnp.concatenate([rms_norm(conv_out, g_conv[l]), rms_norm(fft_out, g_fft[l])], axis=-1)
        o = merged @ w_out[l]
        x = x + gt_m * rms_norm(o, g_post_mix[l])

        h = rms_norm(x, g_pre_ffn[l]) * (1.0 + sc_f) + sh_f
        ff = (jax.nn.silu(h @ w_gate[l]) * (h @ w_up[l])) @ w_down[l]
        x = x + gt_f * rms_norm(ff, g_post_ffn[l])
    return x


def setup_inputs(seed: int = 0) -> dict:
    key = jax.random.key(seed)
    ks = jax.random.split(key, 20)
    f32 = jnp.float32
    nrm = lambda k, s, scale: jax.random.normal(k, s, f32) * scale
    gain = lambda k, s: 1.0 + 0.05 * jax.random.normal(k, s, f32)
    return {
        "x_prompt": nrm(ks[0], (BATCH, SEQ, D_MODEL), 1.0),
        "x_sample": nrm(ks[1], (DEC_BATCH, DEC_SEQ, D_MODEL), 1.0),
        "c_prompt": nrm(ks[2], (BATCH, D_MODEL), 1.0),
        "c_sample": nrm(ks[3], (DEC_BATCH, D_MODEL), 1.0),
        "w_ada": nrm(ks[4], (DEPTH, D_MODEL, N_MOD * D_MODEL), 0.3 * D_MODEL ** -0.5),
        "b_ada": nrm(ks[5], (DEPTH, N_MOD * D_MODEL), 0.02),
        "g_pre_mix": gain(ks[6], (DEPTH, D_MODEL)),
        "g_post_mix": gain(ks[7], (DEPTH, D_MODEL)),
        "w_in": nrm(ks[8], (DEPTH, D_MODEL, IN_WIDTH), D_MODEL ** -0.5),
        "conv_w": nrm(ks[9], (DEPTH, CONV_WIDTH, D_CONV), CONV_WIDTH ** -0.5),
        "conv_b": nrm(ks[10], (DEPTH, D_CONV), 0.01),
        "g_conv": gain(ks[11], (DEPTH, D_CONV)),
        "g_fft": gain(ks[12], (DEPTH, D_FFT)),
        "w_out": nrm(ks[13], (DEPTH, MIX_WIDTH, D_MODEL), MIX_WIDTH ** -0.5),
        "g_pre_ffn": gain(ks[14], (DEPTH, D_MODEL)),
        "g_post_ffn": gain(ks[15], (DEPTH, D_MODEL)),
        "w_gate": nrm(ks[16], (DEPTH, D_MODEL, D_FF), D_MODEL ** -0.5),
        "w_up": nrm(ks[17], (DEPTH, D_MODEL, D_FF), D_MODEL ** -0.5),
        "w_down": nrm(ks[18], (DEPTH, D_FF, D_MODEL), D_FF ** -0.5),
    }


def reference(x_prompt, x_sample, c_prompt, c_sample, w_ada, b_ada, g_pre_mix, g_post_mix,
              w_in, conv_w, conv_b, g_conv, g_fft, w_out, g_pre_ffn, g_post_ffn,
              w_gate, w_up, w_down):
    y_prompt = run_trunk(x_prompt, c_prompt, w_ada, b_ada, g_pre_mix, g_post_mix, w_in, conv_w,
                         conv_b, g_conv, g_fft, w_out, g_pre_ffn, g_post_ffn, w_gate, w_up, w_down)
    y_sample = run_trunk(x_sample, c_sample, w_ada, b_ada, g_pre_mix, g_post_mix, w_in, conv_w,
                         conv_b, g_conv, g_fft, w_out, g_pre_ffn, g_post_ffn, w_gate, w_up, w_down)
    return (y_prompt, y_sample)
```

```python
import functools
import math

import jax
import jax.numpy as jnp
from jax import lax
from jax.experimental import pallas as pl
from jax.experimental.pallas import tpu as pltpu

D_MODEL = 1024
D_CONV = 512
D_FFT = 512
HEAD_DIM = 64
N_FFT_GROUPS = D_FFT // HEAD_DIM
D_FF = 2816
N_MOD = 6
EPS = 1e-6
IN_EXT = 3 * D_CONV + 2 * D_FFT

F32 = jnp.float32
BF16 = jnp.bfloat16

V7X_VMEM_BYTES = 64 * 1024 * 1024
VMEM_LIMIT = V7X_VMEM_BYTES - 8 * 1024 * 1024

TOKEN_TILE = 512
DFT_ROW_TILE = 512
TABLE_ROW_TILE = 256
FF_CHUNK = 768
CONV_HALO = 8


def _rms(v, g):
    return v * lax.rsqrt(jnp.mean(v * v, axis=-1, keepdims=True) + EPS) * g


def _resident(block_shape, index_map):
    return pl.BlockSpec(block_shape, index_map, pipeline_mode=pl.Buffered(1))


def _params(sem):
    return pltpu.CompilerParams(dimension_semantics=sem, vmem_limit_bytes=VMEM_LIMIT)


def _dft_table_body(o_ref, *, seq, rows):
    k = lax.broadcasted_iota(jnp.int32, (rows, seq), 0) + pl.program_id(0) * rows
    s = lax.broadcasted_iota(jnp.int32, (rows, seq), 1)
    ang = ((k * s) & (seq - 1)).astype(F32) * (2.0 * math.pi / seq)
    o_ref[:, :seq] = jnp.cos(ang).astype(BF16)
    o_ref[:, seq:] = (-jnp.sin(ang)).astype(BF16)


def _dft_table(seq):
    rows = TABLE_ROW_TILE
    return pl.pallas_call(
        functools.partial(_dft_table_body, seq=seq, rows=rows),
        grid=(seq // rows,),
        out_specs=pl.BlockSpec((rows, 2 * seq), lambda i: (i, 0)),
        out_shape=jax.ShapeDtypeStruct((seq, 2 * seq), BF16),
        compiler_params=_params(("arbitrary",)),
        name=f"dft_table_{seq}",
    )()


def _fold_body(w_ref, o_ref):
    c = lax.broadcasted_iota(jnp.int32, (HEAD_DIM, 2 * HEAD_DIM), 0)
    m = lax.broadcasted_iota(jnp.int32, (HEAD_DIM, 2 * HEAD_DIM), 1)
    ang = ((c * (m & (HEAD_DIM - 1))) & (HEAD_DIM - 1)).astype(F32) * (2.0 * math.pi / HEAD_DIM)
    tab = jnp.where(m < HEAD_DIM, jnp.cos(ang), jnp.sin(ang))
    o_ref[...] = jnp.dot(w_ref[...], tab, precision=lax.Precision.HIGHEST,
                         preferred_element_type=F32)


def _fold_channel_dft(w_f):
    depth = w_f.shape[0]
    rows = depth * D_MODEL * N_FFT_GROUPS
    tile = 4096
    flat = w_f.reshape(rows, HEAD_DIM)
    out = pl.pallas_call(
        _fold_body,
        grid=(rows // tile,),
        in_specs=[pl.BlockSpec((tile, HEAD_DIM), lambda i: (i, 0))],
        out_specs=pl.BlockSpec((tile, 2 * HEAD_DIM), lambda i: (i, 0)),
        out_shape=jax.ShapeDtypeStruct((rows, 2 * HEAD_DIM), F32),
        compiler_params=_params(("arbitrary",)),
        name="fold_channel_dft",
    )(flat)
    out = out.reshape(depth, D_MODEL, N_FFT_GROUPS, 2, HEAD_DIM)
    return out.transpose(0, 1, 3, 2, 4).reshape(depth, D_MODEL, 2 * D_FFT)


def _mod_body(c_ref, w_ref, b_ref, o_ref):
    c = c_ref[...]
    act = c * jax.nn.sigmoid(c)
    o_ref[0] = jnp.dot(act, w_ref[0], precision=lax.Precision.HIGHEST,
                       preferred_element_type=F32) + b_ref[0]


def _modulation(c_all, w_ada, b_ada):
    depth = w_ada.shape[0]
    rows = c_all.shape[0]
    return pl.pallas_call(
        _mod_body,
        grid=(depth, N_MOD),
        in_specs=[
            pl.BlockSpec((rows, D_MODEL), lambda l, j: (0, 0)),
            pl.BlockSpec((1, D_MODEL, D_MODEL), lambda l, j: (l, 0, j)),
            pl.BlockSpec((1, 1, D_MODEL), lambda l, j: (l, 0, j)),
        ],
        out_specs=pl.BlockSpec((1, rows, D_MODEL), lambda l, j: (l, 0, j)),
        out_shape=jax.ShapeDtypeStruct((depth, rows, N_MOD * D_MODEL), F32),
        compiler_params=_params(("arbitrary", "arbitrary")),
        name="adaln_modulation",
    )(c_all, w_ada, b_ada.reshape(depth, 1, N_MOD * D_MODEL))


def _mix_in_body(x_ref, mod_ref, g_ref, w_ref, bg_ref, u_ref, a_ref):
    d = D_MODEL
    sh = mod_ref[:, 0 * d:1 * d]
    sc = mod_ref[:, 1 * d:2 * d]
    h = _rms(x_ref[0], g_ref[...]) * (1.0 + sc) + sh
    z = jnp.dot(h.astype(BF16), w_ref[...], preferred_element_type=F32)
    bg_ref[0] = z[:, :D_CONV]
    u_ref[0] = z[:, D_CONV:2 * D_CONV] * z[:, 2 * D_CONV:3 * D_CONV]
    a_ref[0, 0] = z[:, 3 * D_CONV:3 * D_CONV + D_FFT].astype(BF16)
    a_ref[0, 1] = z[:, 3 * D_CONV + D_FFT:].astype(BF16)


def _mix_in(x, mod_all, g_pre, w_in_ext, layer, row0):
    bsz, seq, d = x.shape
    tm = TOKEN_TILE
    return pl.pallas_call(
        _mix_in_body,
        grid=(bsz, seq // tm),
        in_specs=[
            pl.BlockSpec((1, tm, d), lambda b, i: (b, i, 0)),
            pl.BlockSpec((None, None, 1, N_MOD * d), lambda b, i: (layer, row0 + b, 0, 0)),
            _resident((None, 1, d), lambda b, i: (layer, 0, 0)),
            _resident((None, d, IN_EXT), lambda b, i: (layer, 0, 0)),
        ],
        out_specs=[
            pl.BlockSpec((1, tm, D_CONV), lambda b, i: (b, i, 0)),
            pl.BlockSpec((1, tm, D_CONV), lambda b, i: (b, i, 0)),
            pl.BlockSpec((1, 2, tm, D_FFT), lambda b, i: (b, 0, i, 0)),
        ],
        out_shape=[
            jax.ShapeDtypeStruct((bsz, seq, D_CONV), F32),
            jax.ShapeDtypeStruct((bsz, seq, D_CONV), F32),
            jax.ShapeDtypeStruct((bsz, 2, seq, D_FFT), BF16),
        ],
        compiler_params=_params(("arbitrary", "arbitrary")),
        name="mix_in",
    )(x, mod_all, g_pre, w_in_ext)


def _seq_dft_body(t_ref, a_ref, g_ref, o_ref):
    p = jnp.dot(t_ref[...], a_ref[0], preferred_element_type=F32)
    o_ref[0] = _rms(p, g_ref[...]).astype(BF16)


def _seq_dft(table, a, g_fft, layer):
    bsz, _, seq, _ = a.shape
    a2 = a.reshape(bsz, 2 * seq, D_FFT)
    tk = DFT_ROW_TILE
    return pl.pallas_call(
        _seq_dft_body,
        grid=(seq // tk, bsz),
        in_specs=[
            pl.BlockSpec((tk, 2 * seq), lambda i, b: (i, 0)),
            pl.BlockSpec((1, 2 * seq, D_FFT), lambda i, b: (b, 0, 0)),
            _resident((None, 1, D_FFT), lambda i, b: (layer, 0, 0)),
        ],
        out_specs=pl.BlockSpec((1, tk, D_FFT), lambda i, b: (b, i, 0)),
        out_shape=jax.ShapeDtypeStruct((bsz, seq, D_FFT), BF16),
        compiler_params=_params(("arbitrary", "arbitrary")),
        name="seq_dft",
    )(table, a2, g_fft)


def _mix_out_body(x_ref, bg_ref, u_ref, up_ref, un_ref, fn_ref, mod_ref, cw_ref, cb_ref,
                  gc_ref, gpm_ref, wo_ref, gpf_ref, gpo_ref, wg_ref, wu_ref, wd_ref, o_ref):
    d = D_MODEL
    tm = u_ref.shape[1]
    i = pl.program_id(1)
    last = pl.num_programs(1) - 1

    u = u_ref[0]
    prev = jnp.where(i > 0, up_ref[0, CONV_HALO - 1:CONV_HALO, :], 0.0)
    nxt = jnp.where(i < last, un_ref[0, 0:1, :], 0.0)
    row = lax.broadcasted_iota(jnp.int32, u.shape, 0)
    u_m1 = jnp.where(row == 0, prev, pltpu.roll(u, 1, 0))
    u_p1 = jnp.where(row == tm - 1, nxt, pltpu.roll(u, tm - 1, 0))
    conv = u_m1 * cw_ref[0:1, :] + u * cw_ref[1:2, :] + u_p1 * cw_ref[2:3, :] + cb_ref[...]
    conv_n = _rms(bg_ref[0] * conv, gc_ref[...]).astype(BF16)

    merged = jnp.concatenate([conv_n, fn_ref[0]], axis=-1)
    o = jnp.dot(merged, wo_ref[...], preferred_element_type=F32)
    x1 = x_ref[0] + mod_ref[:, 2 * d:3 * d] * _rms(o, gpm_ref[...])

    h = _rms(x1, gpf_ref[...]) * (1.0 + mod_ref[:, 4 * d:5 * d]) + mod_ref[:, 3 * d:4 * d]
    hb = h.astype(BF16)
    ff = None
    for c0 in range(0, D_FF, FF_CHUNK):
        c1 = min(c0 + FF_CHUNK, D_FF)
        gate = jnp.dot(hb, wg_ref[:, c0:c1], preferred_element_type=F32)
        up = jnp.dot(hb, wu_ref[:, c0:c1], preferred_element_type=F32)
        act = (gate * jax.nn.sigmoid(gate) * up).astype(BF16)
        part = jnp.dot(act, wd_ref[c0:c1, :], preferred_element_type=F32)
        ff = part if ff is None else ff + part
    o_ref[0] = x1 + mod_ref[:, 5 * d:6 * d] * _rms(ff, gpo_ref[...])


def _mix_out(x, bg, u, fn, mod_all, conv_w, conv_b, g_conv, g_post_mix, w_out, g_pre_ffn,
             g_post_ffn, w_gate, w_up, w_down, layer, row0):
    bsz, seq, d = x.shape
    tm = TOKEN_TILE
    halo_per_tile = tm // CONV_HALO
    n_halo = seq // CONV_HALO
    tok = lambda b, i: (b, i, 0)
    lay = lambda b, i: (layer, 0, 0)
    return pl.pallas_call(
        _mix_out_body,
        grid=(bsz, seq // tm),
        in_specs=[
            pl.BlockSpec((1, tm, d), tok),
            pl.BlockSpec((1, tm, D_CONV), tok),
            pl.BlockSpec((1, tm, D_CONV), tok),
            pl.BlockSpec((1, CONV_HALO, D_CONV),
                         lambda b, i: (b, jnp.maximum(i * halo_per_tile - 1, 0), 0)),
            pl.BlockSpec((1, CONV_HALO, D_CONV),
                         lambda b, i: (b, jnp.minimum((i + 1) * halo_per_tile, n_halo - 1), 0)),
            pl.BlockSpec((1, tm, D_FFT), tok),
            pl.BlockSpec((None, None, 1, N_MOD * d), lambda b, i: (layer, row0 + b, 0, 0)),
            _resident((None, 3, D_CONV), lay),
            _resident((None, 1, D_CONV), lay),
            _resident((None, 1, D_CONV), lay),
            _resident((None, 1, d), lay),
            _resident((None, d, d), lay),
            _resident((None, 1, d), lay),
            _resident((None, 1, d), lay),
            _resident((None, d, D_FF), lay),
            _resident((None, d, D_FF), lay),
            _resident((None, D_FF, d), lay),
        ],
        out_specs=pl.BlockSpec((1, tm, d), tok),
        out_shape=jax.ShapeDtypeStruct((bsz, seq, d), F32),
        compiler_params=_params(("arbitrary", "arbitrary")),
        name="mix_out",
    )(x, bg, u, u, u, fn, mod_all, conv_w, conv_b, g_conv, g_post_mix, w_out, g_pre_ffn,
      g_post_ffn, w_gate, w_up, w_down)


def kernel(x_prompt, x_sample, c_prompt, c_sample, w_ada, b_ada, g_pre_mix, g_post_mix, w_in,
           conv_w, conv_b, g_conv, g_fft, w_out, g_pre_ffn, g_post_ffn, w_gate, w_up, w_down):
    depth = w_in.shape[0]
    row3 = lambda g: g.reshape(depth, 1, g.shape[-1])

    w_fold = _fold_channel_dft(w_in[:, :, 3 * D_CONV:])
    w_in_ext = jnp.concatenate([w_in[:, :, :3 * D_CONV], w_fold], axis=-1).astype(BF16)
    w_out_b = w_out.astype(BF16)
    w_gate_b = w_gate.astype(BF16)
    w_up_b = w_up.astype(BF16)
    w_down_b = w_down.astype(BF16)

    c_all = jnp.concatenate([c_prompt, c_sample], axis=0)
    mod_all = _modulation(c_all, w_ada, b_ada)
    mod_all = mod_all.reshape(depth, c_all.shape[0], 1, N_MOD * D_MODEL)

    g_pre_mix3, g_post_mix3 = row3(g_pre_mix), row3(g_post_mix)
    g_conv3, g_fft3, conv_b3 = row3(g_conv), row3(g_fft), row3(conv_b)
    g_pre_ffn3, g_post_ffn3 = row3(g_pre_ffn), row3(g_post_ffn)

    outs = []
    for x, row0 in ((x_prompt, 0), (x_sample, c_prompt.shape[0])):
        table = _dft_table(x.shape[1])
        for layer in range(depth):
            bg, u, a = _mix_in(x, mod_all, g_pre_mix3, w_in_ext, layer, row0)
            fn = _seq_dft(table, a, g_fft3, layer)
            x = _mix_out(x, bg, u, fn, mod_all, conv_w, conv_b3, g_conv3, g_post_mix3, w_out_b,
                         g_pre_ffn3, g_post_ffn3, w_gate_b, w_up_b, w_down_b, layer, row0)
        outs.append(x)
    return tuple(outs)
```

```python
import functools
import math

import jax
import jax.numpy as jnp
from jax import lax
from jax.experimental import pallas as pl
from jax.experimental.pallas import tpu as pltpu

D_MODEL = 1024
D_CONV = 512
D_FFT = 512
HEAD_DIM = 64
N_FFT_GROUPS = D_FFT // HEAD_DIM
D_FF = 2816
N_MOD = 6
EPS = 1e-6
IN_EXT = 3 * D_CONV + 2 * D_FFT

F32 = jnp.float32
BF16 = jnp.bfloat16

V7X_VMEM_BYTES = 64 * 1024 * 1024
VMEM_LIMIT = V7X_VMEM_BYTES - 8 * 1024 * 1024

TOKEN_TILE = 512
BF16_SUBLANES = 16
FFT_N1 = BF16_SUBLANES
FFT_ROWS = BF16_SUBLANES
FF_CHUNK = 768
CONV_HALO = 8


def _rms(v, g):
    return v * lax.rsqrt(jnp.mean(v * v, axis=-1, keepdims=True) + EPS) * g


def _resident(block_shape, index_map):
    return pl.BlockSpec(block_shape, index_map, pipeline_mode=pl.Buffered(1))


def _params(sem):
    return pltpu.CompilerParams(dimension_semantics=sem, vmem_limit_bytes=VMEM_LIMIT)


def _fft_tables_body(ka_ref, db_ref, tw_ref, *, n1, n2):
    half = n1 * FFT_ROWS
    i = lax.broadcasted_iota(jnp.int32, (2 * half, 2 * half), 0)
    j = lax.broadcasted_iota(jnp.int32, (2 * half, 2 * half), 1)
    k1 = (i & (half - 1)) // FFT_ROWS
    s1 = (j & (half - 1)) // FFT_ROWS
    same_r = (i & (FFT_ROWS - 1)) == (j & (FFT_ROWS - 1))
    ang = ((k1 * s1) & (n1 - 1)).astype(F32) * (2.0 * math.pi / n1)
    pi_, pj = i >= half, j >= half
    val = jnp.where(pi_ == pj, jnp.cos(ang), jnp.where(pj, -jnp.sin(ang), jnp.sin(ang)))
    ka_ref[...] = jnp.where(same_r, val, 0.0).astype(BF16)

    k2 = lax.broadcasted_iota(jnp.int32, (n2, 2 * n2), 0)
    q = lax.broadcasted_iota(jnp.int32, (n2, 2 * n2), 1)
    ang = ((k2 * (q & (n2 - 1))) & (n2 - 1)).astype(F32) * (2.0 * math.pi / n2)
    db_ref[...] = jnp.where(q < n2, jnp.cos(ang), -jnp.sin(ang)).astype(BF16)

    t = lax.broadcasted_iota(jnp.int32, (n1 * n2, 128), 0)
    ang = ((t // n2) * (t & (n2 - 1))).astype(F32) * (2.0 * math.pi / (n1 * n2))
    tw_ref[0] = jnp.cos(ang)
    tw_ref[1] = jnp.sin(ang)


def _fft_tables(seq):
    n1, n2 = FFT_N1, seq // FFT_N1
    half = n1 * FFT_ROWS
    return pl.pallas_call(
        functools.partial(_fft_tables_body, n1=n1, n2=n2),
        out_shape=[
            jax.ShapeDtypeStruct((2 * half, 2 * half), BF16),
            jax.ShapeDtypeStruct((n2, 2 * n2), BF16),
            jax.ShapeDtypeStruct((2, seq, 128), F32),
        ],
        compiler_params=pltpu.CompilerParams(vmem_limit_bytes=VMEM_LIMIT),
        name=f"fft_tables_{seq}",
    )()


def _fold_body(w_ref, o_ref):
    c = lax.broadcasted_iota(jnp.int32, (HEAD_DIM, 2 * HEAD_DIM), 0)
    m = lax.broadcasted_iota(jnp.int32, (HEAD_DIM, 2 * HEAD_DIM), 1)
    ang = ((c * (m & (HEAD_DIM - 1))) & (HEAD_DIM - 1)).astype(F32) * (2.0 * math.pi / HEAD_DIM)
    tab = jnp.where(m < HEAD_DIM, jnp.cos(ang), jnp.sin(ang))
    o_ref[...] = jnp.dot(w_ref[...], tab, precision=lax.Precision.HIGHEST,
                         preferred_element_type=F32)


def _fold_channel_dft(w_f):
    depth = w_f.shape[0]
    rows = depth * D_MODEL * N_FFT_GROUPS
    tile = 4096
    flat = w_f.reshape(rows, HEAD_DIM)
    out = pl.pallas_call(
        _fold_body,
        grid=(rows // tile,),
        in_specs=[pl.BlockSpec((tile, HEAD_DIM), lambda i: (i, 0))],
        out_specs=pl.BlockSpec((tile, 2 * HEAD_DIM), lambda i: (i, 0)),
        out_shape=jax.ShapeDtypeStruct((rows, 2 * HEAD_DIM), F32),
        compiler_params=_params(("arbitrary",)),
        name="fold_channel_dft",
    )(flat)
    out = out.reshape(depth, D_MODEL, N_FFT_GROUPS, 2, HEAD_DIM)
    return out.transpose(0, 1, 3, 2, 4).reshape(depth, D_MODEL, 2 * D_FFT)


def _mod_body(c_ref, w_ref, b_ref, o_ref):
    c = c_ref[...]
    act = c * jax.nn.sigmoid(c)
    o_ref[0] = jnp.dot(act, w_ref[0], precision=lax.Precision.HIGHEST,
                       preferred_element_type=F32) + b_ref[0]


def _modulation(c_all, w_ada, b_ada):
    depth = w_ada.shape[0]
    rows = c_all.shape[0]
    return pl.pallas_call(
        _mod_body,
        grid=(depth, N_MOD),
        in_specs=[
            pl.BlockSpec((rows, D_MODEL), lambda l, j: (0, 0)),
            pl.BlockSpec((1, D_MODEL, D_MODEL), lambda l, j: (l, 0, j)),
            pl.BlockSpec((1, 1, D_MODEL), lambda l, j: (l, 0, j)),
        ],
        out_specs=pl.BlockSpec((1, rows, D_MODEL), lambda l, j: (l, 0, j)),
        out_shape=jax.ShapeDtypeStruct((depth, rows, N_MOD * D_MODEL), F32),
        compiler_params=_params(("arbitrary", "arbitrary")),
        name="adaln_modulation",
    )(c_all, w_ada, b_ada.reshape(depth, 1, N_MOD * D_MODEL))


def _mix_in_body(x_ref, mod_ref, g_ref, w_ref, bg_ref, u_ref, a_ref):
    d = D_MODEL
    sh = mod_ref[:, 0 * d:1 * d]
    sc = mod_ref[:, 1 * d:2 * d]
    h = _rms(x_ref[0], g_ref[...]) * (1.0 + sc) + sh
    z = jnp.dot(h.astype(BF16), w_ref[...], preferred_element_type=F32)
    bg_ref[0] = z[:, :D_CONV]
    u_ref[0] = z[:, D_CONV:2 * D_CONV] * z[:, 2 * D_CONV:3 * D_CONV]
    a_ref[0, 0] = z[:, 3 * D_CONV:3 * D_CONV + D_FFT].astype(BF16)
    a_ref[0, 1] = z[:, 3 * D_CONV + D_FFT:].astype(BF16)


def _mix_in(x, mod_all, g_pre, w_in_ext, layer, row0):
    bsz, seq, d = x.shape
    tm = TOKEN_TILE
    return pl.pallas_call(
        _mix_in_body,
        grid=(bsz, seq // tm),
        in_specs=[
            pl.BlockSpec((1, tm, d), lambda b, i: (b, i, 0)),
            pl.BlockSpec((None, None, 1, N_MOD * d), lambda b, i: (layer, row0 + b, 0, 0)),
            _resident((None, 1, d), lambda b, i: (layer, 0, 0)),
            _resident((None, d, IN_EXT), lambda b, i: (layer, 0, 0)),
        ],
        out_specs=[
            pl.BlockSpec((1, tm, D_CONV), lambda b, i: (b, i, 0)),
            pl.BlockSpec((1, tm, D_CONV), lambda b, i: (b, i, 0)),
            pl.BlockSpec((1, 2, tm, D_FFT), lambda b, i: (b, 0, i, 0)),
        ],
        out_shape=[
            jax.ShapeDtypeStruct((bsz, seq, D_CONV), F32),
            jax.ShapeDtypeStruct((bsz, seq, D_CONV), F32),
            jax.ShapeDtypeStruct((bsz, 2, seq, D_FFT), BF16),
        ],
        compiler_params=_params(("arbitrary", "arbitrary")),
        name="mix_in",
    )(x, mod_all, g_pre, w_in_ext)


def _seq_fft_body(a_ref, ka_ref, db_ref, tw_ref, g_ref, o_ref, y_ref, f_ref, *, n1, n2):
    rows = FFT_ROWS
    half = n1 * rows
    lanes = tw_ref.shape[-1]
    for j in range(n2 // rows):
        r0 = j * rows
        x = jnp.concatenate([a_ref[0, 0, :, r0:r0 + rows, :].reshape(half, D_FFT),
                             a_ref[0, 1, :, r0:r0 + rows, :].reshape(half, D_FFT)], axis=0)
        y = jnp.dot(ka_ref[...], x, preferred_element_type=F32)
        cw = tw_ref[0, :, r0:r0 + rows, :].reshape(half, lanes)
        sw = tw_ref[1, :, r0:r0 + rows, :].reshape(half, lanes)
        yr_cols, yi_cols = [], []
        for c0 in range(0, D_FFT, lanes):
            yr, yi = y[:half, c0:c0 + lanes], y[half:, c0:c0 + lanes]
            yr_cols.append(yr * cw - yi * sw)
            yi_cols.append(yr * sw + yi * cw)
        y_ref[:, 0, r0:r0 + rows, :] = (
            jnp.concatenate(yr_cols, axis=1).astype(BF16).reshape(n1, rows, D_FFT))
        y_ref[:, 1, r0:r0 + rows, :] = (
            jnp.concatenate(yi_cols, axis=1).astype(BF16).reshape(n1, rows, D_FFT))

    for k1 in range(n1):
        f = jnp.dot(db_ref[...], y_ref[k1].reshape(2 * n2, D_FFT), preferred_element_type=F32)
        fn = _rms(f, g_ref[...])
        for cb in range(D_FFT // lanes):
            f_ref[cb, k1 * n2:(k1 + 1) * n2, :] = fn[:, cb * lanes:(cb + 1) * lanes]

    def to_sequence_order(k2, carry):
        block = jnp.concatenate([f_ref[cb, pl.ds(k2, n1, stride=n2), :]
                                 for cb in range(D_FFT // lanes)], axis=1)
        o_ref[0, pl.ds(pl.multiple_of(k2 * n1, n1), n1), :] = block.astype(BF16)
        return carry

    lax.fori_loop(0, n2, to_sequence_order, 0, unroll=8)


def _seq_fft(tables, a, g_fft, layer):
    ka, db, tw = tables
    bsz, _, seq, _ = a.shape
    n1, n2 = FFT_N1, seq // FFT_N1
    a5 = a.reshape(bsz, 2, n1, n2, D_FFT)
    tw4 = tw.reshape(2, n1, n2, tw.shape[-1])
    return pl.pallas_call(
        functools.partial(_seq_fft_body, n1=n1, n2=n2),
        grid=(bsz,),
        in_specs=[
            pl.BlockSpec((1, 2, n1, n2, D_FFT), lambda b: (b, 0, 0, 0, 0)),
            _resident(ka.shape, lambda b: (0, 0)),
            _resident(db.shape, lambda b: (0, 0)),
            _resident(tw4.shape, lambda b: (0, 0, 0, 0)),
            _resident((None, 1, D_FFT), lambda b: (layer, 0, 0)),
        ],
        out_specs=pl.BlockSpec((1, seq, D_FFT), lambda b: (b, 0, 0)),
        out_shape=jax.ShapeDtypeStruct((bsz, seq, D_FFT), BF16),
        scratch_shapes=[
            pltpu.VMEM((n1, 2, n2, D_FFT), BF16),
            pltpu.VMEM((D_FFT // tw.shape[-1], seq, tw.shape[-1]), F32),
        ],
        compiler_params=_params(("arbitrary",)),
        name="seq_fft",
    )(a5, ka, db, tw4, g_fft)


def _mix_out_body(x_ref, bg_ref, u_ref, up_ref, un_ref, fn_ref, mod_ref, cw_ref, cb_ref,
                  gc_ref, gpm_ref, wo_ref, gpf_ref, gpo_ref, wg_ref, wu_ref, wd_ref, o_ref):
    d = D_MODEL
    tm = u_ref.shape[1]
    i = pl.program_id(1)
    last = pl.num_programs(1) - 1

    u = u_ref[0]
    prev = jnp.where(i > 0, up_ref[0, CONV_HALO - 1:CONV_HALO, :], 0.0)
    nxt = jnp.where(i < last, un_ref[0, 0:1, :], 0.0)
    row = lax.broadcasted_iota(jnp.int32, u.shape, 0)
    u_m1 = jnp.where(row == 0, prev, pltpu.roll(u, 1, 0))
    u_p1 = jnp.where(row == tm - 1, nxt, pltpu.roll(u, tm - 1, 0))
    conv = u_m1 * cw_ref[0:1, :] + u * cw_ref[1:2, :] + u_p1 * cw_ref[2:3, :] + cb_ref[...]
    conv_n = _rms(bg_ref[0] * conv, gc_ref[...]).astype(BF16)

    merged = jnp.concatenate([conv_n, fn_ref[0]], axis=-1)
    o = jnp.dot(merged, wo_ref[...], preferred_element_type=F32)
    x1 = x_ref[0] + mod_ref[:, 2 * d:3 * d] * _rms(o, gpm_ref[...])

    h = _rms(x1, gpf_ref[...]) * (1.0 + mod_ref[:, 4 * d:5 * d]) + mod_ref[:, 3 * d:4 * d]
    hb = h.astype(BF16)
    ff = None
    for c0 in range(0, D_FF, FF_CHUNK):
        c1 = min(c0 + FF_CHUNK, D_FF)
        gate = jnp.dot(hb, wg_ref[:, c0:c1], preferred_element_type=F32)
        up = jnp.dot(hb, wu_ref[:, c0:c1], preferred_element_type=F32)
        act = (gate * jax.nn.sigmoid(gate) * up).astype(BF16)
        part = jnp.dot(act, wd_ref[c0:c1, :], preferred_element_type=F32)
        ff = part if ff is None else ff + part
    o_ref[0] = x1 + mod_ref[:, 5 * d:6 * d] * _rms(ff, gpo_ref[...])


def _mix_out(x, bg, u, fn, mod_all, conv_w, conv_b, g_conv, g_post_mix, w_out, g_pre_ffn,
             g_post_ffn, w_gate, w_up, w_down, layer, row0):
    bsz, seq, d = x.shape
    tm = TOKEN_TILE
    halo_per_tile = tm // CONV_HALO
    n_halo = seq // CONV_HALO
    tok = lambda b, i: (b, i, 0)
    lay = lambda b, i: (layer, 0, 0)
    return pl.pallas_call(
        _mix_out_body,
        grid=(bsz, seq // tm),
        in_specs=[
            pl.BlockSpec((1, tm, d), tok),
            pl.BlockSpec((1, tm, D_CONV), tok),
            pl.BlockSpec((1, tm, D_CONV), tok),
            pl.BlockSpec((1, CONV_HALO, D_CONV),
                         lambda b, i: (b, jnp.maximum(i * halo_per_tile - 1, 0), 0)),
            pl.BlockSpec((1, CONV_HALO, D_CONV),
                         lambda b, i: (b, jnp.minimum((i + 1) * halo_per_tile, n_halo - 1), 0)),
            pl.BlockSpec((1, tm, D_FFT), tok),
            pl.BlockSpec((None, None, 1, N_MOD * d), lambda b, i: (layer, row0 + b, 0, 0)),
            _resident((None, 3, D_CONV), lay),
            _resident((None, 1, D_CONV), lay),
            _resident((None, 1, D_CONV), lay),
            _resident((None, 1, d), lay),
            _resident((None, d, d), lay),
            _resident((None, 1, d), lay),
            _resident((None, 1, d), lay),
            _resident((None, d, D_FF), lay),
            _resident((None, d, D_FF), lay),
            _resident((None, D_FF, d), lay),
        ],
        out_specs=pl.BlockSpec((1, tm, d), tok),
        out_shape=jax.ShapeDtypeStruct((bsz, seq, d), F32),
        compiler_params=_params(("arbitrary", "arbitrary")),
        name="mix_out",
    )(x, bg, u, u, u, fn, mod_all, conv_w, conv_b, g_conv, g_post_mix, w_out, g_pre_ffn,
      g_post_ffn, w_gate, w_up, w_down)


def kernel(x_prompt, x_sample, c_prompt, c_sample, w_ada, b_ada, g_pre_mix, g_post_mix, w_in,
           conv_w, conv_b, g_conv, g_fft, w_out, g_pre_ffn, g_post_ffn, w_gate, w_up, w_down):
    depth = w_in.shape[0]
    row3 = lambda g: g.reshape(depth, 1, g.shape[-1])

    w_fold = _fold_channel_dft(w_in[:, :, 3 * D_CONV:])
    w_in_ext = jnp.concatenate([w_in[:, :, :3 * D_CONV], w_fold], axis=-1).astype(BF16)
    w_out_b = w_out.astype(BF16)
    w_gate_b = w_gate.astype(BF16)
    w_up_b = w_up.astype(BF16)
    w_down_b = w_down.astype(BF16)

    c_all = jnp.concatenate([c_prompt, c_sample], axis=0)
    mod_all = _modulation(c_all, w_ada, b_ada)
    mod_all = mod_all.reshape(depth, c_all.shape[0], 1, N_MOD * D_MODEL)

    g_pre_mix3, g_post_mix3 = row3(g_pre_mix), row3(g_post_mix)
    g_conv3, g_fft3, conv_b3 = row3(g_conv), row3(g_fft), row3(conv_b)
    g_pre_ffn3, g_post_ffn3 = row3(g_pre_ffn), row3(g_post_ffn)

    outs = []
    for x, row0 in ((x_prompt, 0), (x_sample, c_prompt.shape[0])):
        tables = _fft_tables(x.shape[1])
        for layer in range(depth):
            bg, u, a = _mix_in(x, mod_all, g_pre_mix3, w_in_ext, layer, row0)
            fn = _seq_fft(tables, a, g_fft3, layer)
            x = _mix_out(x, bg, u, fn, mod_all, conv_w, conv_b3, g_conv3, g_post_mix3, w_out_b,
                         g_pre_ffn3, g_post_ffn3, w_gate_b, w_up_b, w_down_b, layer, row0)
        outs.append(x)
    return tuple(outs)
```

```python
import functools
import math

import jax
import jax.numpy as jnp
from jax import lax
from jax.experimental import pallas as pl
from jax.experimental.pallas import tpu as pltpu

D_MODEL = 1024
D_CONV = 512
D_FFT = 512
HEAD_DIM = 64
N_FFT_GROUPS = D_FFT // HEAD_DIM
D_FF = 2816
N_MOD = 6
EPS = 1e-6
IN_EXT = 3 * D_CONV + 2 * D_FFT

F32 = jnp.float32
BF16 = jnp.bfloat16

V7X_VMEM_BYTES = 64 * 1024 * 1024
VMEM_LIMIT = V7X_VMEM_BYTES - 8 * 1024 * 1024

TOKEN_TILE = 512
SUB_TILES = 2
BF16_SUBLANES = 16
FFT_N1 = BF16_SUBLANES
FFT_ROWS = BF16_SUBLANES
FF_CHUNK = 768
CONV_HALO = 8


def _rms(v, g):
    return v * lax.rsqrt(jnp.mean(v * v, axis=-1, keepdims=True) + EPS) * g


def _resident(block_shape, index_map):
    return pl.BlockSpec(block_shape, index_map, pipeline_mode=pl.Buffered(1))


def _params(sem):
    return pltpu.CompilerParams(dimension_semantics=sem, vmem_limit_bytes=VMEM_LIMIT)


def _fft_tables_body(ka_ref, db_ref, tw_ref, *, n1, n2):
    half = n1 * FFT_ROWS
    i = lax.broadcasted_iota(jnp.int32, (2 * half, 2 * half), 0)
    j = lax.broadcasted_iota(jnp.int32, (2 * half, 2 * half), 1)
    k1 = (i & (half - 1)) // FFT_ROWS
    s1 = (j & (half - 1)) // FFT_ROWS
    same_r = (i & (FFT_ROWS - 1)) == (j & (FFT_ROWS - 1))
    ang = ((k1 * s1) & (n1 - 1)).astype(F32) * (2.0 * math.pi / n1)
    pi_, pj = i >= half, j >= half
    val = jnp.where(pi_ == pj, jnp.cos(ang), jnp.where(pj, -jnp.sin(ang), jnp.sin(ang)))
    ka_ref[...] = jnp.where(same_r, val, 0.0).astype(BF16)

    k2 = lax.broadcasted_iota(jnp.int32, (n2, 2 * n2), 0)
    q = lax.broadcasted_iota(jnp.int32, (n2, 2 * n2), 1)
    ang = ((k2 * (q & (n2 - 1))) & (n2 - 1)).astype(F32) * (2.0 * math.pi / n2)
    db_ref[...] = jnp.where(q < n2, jnp.cos(ang), -jnp.sin(ang)).astype(BF16)

    t = lax.broadcasted_iota(jnp.int32, (n1 * n2, 128), 0)
    ang = ((t // n2) * (t & (n2 - 1))).astype(F32) * (2.0 * math.pi / (n1 * n2))
    tw_ref[0] = jnp.cos(ang)
    tw_ref[1] = jnp.sin(ang)


def _fft_tables(seq):
    n1, n2 = FFT_N1, seq // FFT_N1
    half = n1 * FFT_ROWS
    return pl.pallas_call(
        functools.partial(_fft_tables_body, n1=n1, n2=n2),
        out_shape=[
            jax.ShapeDtypeStruct((2 * half, 2 * half), BF16),
            jax.ShapeDtypeStruct((n2, 2 * n2), BF16),
            jax.ShapeDtypeStruct((2, seq, 128), F32),
        ],
        compiler_params=pltpu.CompilerParams(vmem_limit_bytes=VMEM_LIMIT),
        name=f"fft_tables_{seq}",
    )()


def _fold_body(w_ref, o_ref):
    c = lax.broadcasted_iota(jnp.int32, (HEAD_DIM, 2 * HEAD_DIM), 0)
    m = lax.broadcasted_iota(jnp.int32, (HEAD_DIM, 2 * HEAD_DIM), 1)
    ang = ((c * (m & (HEAD_DIM - 1))) & (HEAD_DIM - 1)).astype(F32) * (2.0 * math.pi / HEAD_DIM)
    tab = jnp.where(m < HEAD_DIM, jnp.cos(ang), jnp.sin(ang))
    o_ref[...] = jnp.dot(w_ref[...], tab, precision=lax.Precision.HIGHEST,
                         preferred_element_type=F32)


def _fold_channel_dft(w_f):
    depth = w_f.shape[0]
    rows = depth * D_MODEL * N_FFT_GROUPS
    tile = 4096
    flat = w_f.reshape(rows, HEAD_DIM)
    out = pl.pallas_call(
        _fold_body,
        grid=(rows // tile,),
        in_specs=[pl.BlockSpec((tile, HEAD_DIM), lambda i: (i, 0))],
        out_specs=pl.BlockSpec((tile, 2 * HEAD_DIM), lambda i: (i, 0)),
        out_shape=jax.ShapeDtypeStruct((rows, 2 * HEAD_DIM), F32),
        compiler_params=_params(("arbitrary",)),
        name="fold_channel_dft",
    )(flat)
    out = out.reshape(depth, D_MODEL, N_FFT_GROUPS, 2, HEAD_DIM)
    return out.transpose(0, 1, 3, 2, 4).reshape(depth, D_MODEL, 2 * D_FFT)


def _mod_body(c_ref, w_ref, b_ref, o_ref):
    c = c_ref[...]
    act = c * jax.nn.sigmoid(c)
    o_ref[0] = jnp.dot(act, w_ref[0], precision=lax.Precision.HIGHEST,
                       preferred_element_type=F32) + b_ref[0]


def _modulation(c_all, w_ada, b_ada):
    depth = w_ada.shape[0]
    rows = c_all.shape[0]
    return pl.pallas_call(
        _mod_body,
        grid=(depth, N_MOD),
        in_specs=[
            pl.BlockSpec((rows, D_MODEL), lambda l, j: (0, 0)),
            pl.BlockSpec((1, D_MODEL, D_MODEL), lambda l, j: (l, 0, j)),
            pl.BlockSpec((1, 1, D_MODEL), lambda l, j: (l, 0, j)),
        ],
        out_specs=pl.BlockSpec((1, rows, D_MODEL), lambda l, j: (l, 0, j)),
        out_shape=jax.ShapeDtypeStruct((depth, rows, N_MOD * D_MODEL), F32),
        compiler_params=_params(("arbitrary", "arbitrary")),
        name="adaln_modulation",
    )(c_all, w_ada, b_ada.reshape(depth, 1, N_MOD * D_MODEL))


def _mix_in_body(x_ref, mod_ref, g_ref, w_ref, bg_ref, u_ref, a_ref):
    d = D_MODEL
    sh = mod_ref[:, 0 * d:1 * d]
    sc = mod_ref[:, 1 * d:2 * d]
    h = _rms(x_ref[0], g_ref[...]) * (1.0 + sc) + sh
    z = jnp.dot(h.astype(BF16), w_ref[...], preferred_element_type=F32)
    bg_ref[0] = z[:, :D_CONV]
    u_ref[0] = z[:, D_CONV:2 * D_CONV] * z[:, 2 * D_CONV:3 * D_CONV]
    a_ref[0, 0] = z[:, 3 * D_CONV:3 * D_CONV + D_FFT].astype(BF16)
    a_ref[0, 1] = z[:, 3 * D_CONV + D_FFT:].astype(BF16)


def _mix_in(x, mod_all, g_pre, w_in_ext, layer, row0):
    bsz, seq, d = x.shape
    tm = TOKEN_TILE
    return pl.pallas_call(
        _mix_in_body,
        grid=(bsz, seq // tm),
        in_specs=[
            pl.BlockSpec((1, tm, d), lambda b, i: (b, i, 0)),
            pl.BlockSpec((None, None, 1, N_MOD * d), lambda b, i: (layer, row0 + b, 0, 0)),
            _resident((None, 1, d), lambda b, i: (layer, 0, 0)),
            _resident((None, d, IN_EXT), lambda b, i: (layer, 0, 0)),
        ],
        out_specs=[
            pl.BlockSpec((1, tm, D_CONV), lambda b, i: (b, i, 0)),
            pl.BlockSpec((1, tm, D_CONV), lambda b, i: (b, i, 0)),
            pl.BlockSpec((1, 2, tm, D_FFT), lambda b, i: (b, 0, i, 0)),
        ],
        out_shape=[
            jax.ShapeDtypeStruct((bsz, seq, D_CONV), F32),
            jax.ShapeDtypeStruct((bsz, seq, D_CONV), F32),
            jax.ShapeDtypeStruct((bsz, 2, seq, D_FFT), BF16),
        ],
        compiler_params=_params(("arbitrary", "arbitrary")),
        name="mix_in",
    )(x, mod_all, g_pre, w_in_ext)


def _seq_fft_body(a_ref, ka_ref, db_ref, tw_ref, g_ref, o_ref, y_ref, f_ref, *, n1, n2):
    rows = FFT_ROWS
    half = n1 * rows
    lanes = tw_ref.shape[-1]
    for j in range(n2 // rows):
        r0 = j * rows
        x = jnp.concatenate([a_ref[0, 0, :, r0:r0 + rows, :].reshape(half, D_FFT),
                             a_ref[0, 1, :, r0:r0 + rows, :].reshape(half, D_FFT)], axis=0)
        y = jnp.dot(ka_ref[...], x, preferred_element_type=F32)
        cw = tw_ref[0, :, r0:r0 + rows, :].reshape(half, lanes)
        sw = tw_ref[1, :, r0:r0 + rows, :].reshape(half, lanes)
        yr_cols, yi_cols = [], []
        for c0 in range(0, D_FFT, lanes):
            yr, yi = y[:half, c0:c0 + lanes], y[half:, c0:c0 + lanes]
            yr_cols.append(yr * cw - yi * sw)
            yi_cols.append(yr * sw + yi * cw)
        y_ref[:, 0, r0:r0 + rows, :] = (
            jnp.concatenate(yr_cols, axis=1).astype(BF16).reshape(n1, rows, D_FFT))
        y_ref[:, 1, r0:r0 + rows, :] = (
            jnp.concatenate(yi_cols, axis=1).astype(BF16).reshape(n1, rows, D_FFT))

    for k1 in range(n1):
        f = jnp.dot(db_ref[...], y_ref[k1].reshape(2 * n2, D_FFT), preferred_element_type=F32)
        fn = _rms(f, g_ref[...])
        for cb in range(D_FFT // lanes):
            f_ref[cb, k1 * n2:(k1 + 1) * n2, :] = fn[:, cb * lanes:(cb + 1) * lanes]

    def to_sequence_order(k2, carry):
        block = jnp.concatenate([f_ref[cb, pl.ds(k2, n1, stride=n2), :]
                                 for cb in range(D_FFT // lanes)], axis=1)
        o_ref[0, pl.ds(pl.multiple_of(k2 * n1, n1), n1), :] = block.astype(BF16)
        return carry

    lax.fori_loop(0, n2, to_sequence_order, 0, unroll=8)


def _seq_fft(tables, a, g_fft, layer):
    ka, db, tw = tables
    bsz, _, seq, _ = a.shape
    n1, n2 = FFT_N1, seq // FFT_N1
    a5 = a.reshape(bsz, 2, n1, n2, D_FFT)
    tw4 = tw.reshape(2, n1, n2, tw.shape[-1])
    return pl.pallas_call(
        functools.partial(_seq_fft_body, n1=n1, n2=n2),
        grid=(bsz,),
        in_specs=[
            pl.BlockSpec((1, 2, n1, n2, D_FFT), lambda b: (b, 0, 0, 0, 0)),
            _resident(ka.shape, lambda b: (0, 0)),
            _resident(db.shape, lambda b: (0, 0)),
            _resident(tw4.shape, lambda b: (0, 0, 0, 0)),
            _resident((None, 1, D_FFT), lambda b: (layer, 0, 0)),
        ],
        out_specs=pl.BlockSpec((1, seq, D_FFT), lambda b: (b, 0, 0)),
        out_shape=jax.ShapeDtypeStruct((bsz, seq, D_FFT), BF16),
        scratch_shapes=[
            pltpu.VMEM((n1, 2, n2, D_FFT), BF16),
            pltpu.VMEM((D_FFT // tw.shape[-1], seq, tw.shape[-1]), F32),
        ],
        compiler_params=_params(("arbitrary",)),
        name="seq_fft",
    )(a5, ka, db, tw4, g_fft)


def _mix_out_body(x_ref, bg_ref, u_ref, up_ref, un_ref, fn_ref, mod_ref, cw_ref, cb_ref,
                  gc_ref, gpm_ref, wo_ref, gpf_ref, gpo_ref, wg_ref, wu_ref, wd_ref, o_ref):
    d = D_MODEL
    tm = u_ref.shape[1]
    ts = tm // SUB_TILES
    i = pl.program_id(1)
    last = pl.num_programs(1) - 1

    def rows_of(sub):
        return slice(sub * ts, (sub + 1) * ts)

    def conv_branch(sub):
        r0 = sub * ts
        u = u_ref[0, rows_of(sub), :]
        if sub == 0:
            prev = jnp.where(i > 0, up_ref[0, CONV_HALO - 1:CONV_HALO, :], 0.0)
        else:
            prev = u_ref[0, r0 - 1:r0, :]
        if sub == SUB_TILES - 1:
            nxt = jnp.where(i < last, un_ref[0, 0:1, :], 0.0)
        else:
            nxt = u_ref[0, r0 + ts:r0 + ts + 1, :]
        row = lax.broadcasted_iota(jnp.int32, u.shape, 0)
        u_m1 = jnp.where(row == 0, prev, pltpu.roll(u, 1, 0))
        u_p1 = jnp.where(row == ts - 1, nxt, pltpu.roll(u, ts - 1, 0))
        conv = u_m1 * cw_ref[0:1, :] + u * cw_ref[1:2, :] + u_p1 * cw_ref[2:3, :] + cb_ref[...]
        conv_n = _rms(bg_ref[0, rows_of(sub), :] * conv, gc_ref[...]).astype(BF16)
        return jnp.concatenate([conv_n, fn_ref[0, rows_of(sub), :]], axis=-1)

    def out_proj(merged):
        return jnp.dot(merged, wo_ref[...], preferred_element_type=F32)

    def residual_and_ffn_input(sub, o):
        x1 = x_ref[0, rows_of(sub), :] + mod_ref[:, 2 * d:3 * d] * _rms(o, gpm_ref[...])
        h = _rms(x1, gpf_ref[...]) * (1.0 + mod_ref[:, 4 * d:5 * d]) + mod_ref[:, 3 * d:4 * d]
        return x1, h.astype(BF16)

    def ffn_chunk(hb, ff, c0):
        c1 = min(c0 + FF_CHUNK, D_FF)
        gate = jnp.dot(hb, wg_ref[:, c0:c1], preferred_element_type=F32)
        up = jnp.dot(hb, wu_ref[:, c0:c1], preferred_element_type=F32)
        act = (gate * jax.nn.sigmoid(gate) * up).astype(BF16)
        part = jnp.dot(act, wd_ref[c0:c1, :], preferred_element_type=F32)
        return part if ff is None else ff + part

    def finish(sub, x1, ff):
        o_ref[0, rows_of(sub), :] = x1 + mod_ref[:, 5 * d:6 * d] * _rms(ff, gpo_ref[...])

    chunks = list(range(0, D_FF, FF_CHUNK))
    x1, hb = residual_and_ffn_input(0, out_proj(conv_branch(0)))
    pending = None
    for sub in range(SUB_TILES):
        more = sub + 1 < SUB_TILES
        if more:
            o_next = out_proj(conv_branch(sub + 1))
        ff = ffn_chunk(hb, None, chunks[0])
        if pending is not None:
            finish(*pending)
        if more:
            x1_next, hb_next = residual_and_ffn_input(sub + 1, o_next)
        for c0 in chunks[1:]:
            ff = ffn_chunk(hb, ff, c0)
        pending = (sub, x1, ff)
        if more:
            x1, hb = x1_next, hb_next
    finish(*pending)


def _mix_out(x, bg, u, fn, mod_all, conv_w, conv_b, g_conv, g_post_mix, w_out, g_pre_ffn,
             g_post_ffn, w_gate, w_up, w_down, layer, row0):
    bsz, seq, d = x.shape
    tm = TOKEN_TILE
    halo_per_tile = tm // CONV_HALO
    n_halo = seq // CONV_HALO
    tok = lambda b, i: (b, i, 0)
    lay = lambda b, i: (layer, 0, 0)
    return pl.pallas_call(
        _mix_out_body,
        grid=(bsz, seq // tm),
        in_specs=[
            pl.BlockSpec((1, tm, d), tok),
            pl.BlockSpec((1, tm, D_CONV), tok),
            pl.BlockSpec((1, tm, D_CONV), tok),
            pl.BlockSpec((1, CONV_HALO, D_CONV),
                         lambda b, i: (b, jnp.maximum(i * halo_per_tile - 1, 0), 0)),
            pl.BlockSpec((1, CONV_HALO, D_CONV),
                         lambda b, i: (b, jnp.minimum((i + 1) * halo_per_tile, n_halo - 1), 0)),
            pl.BlockSpec((1, tm, D_FFT), tok),
            pl.BlockSpec((None, None, 1, N_MOD * d), lambda b, i: (layer, row0 + b, 0, 0)),
            _resident((None, 3, D_CONV), lay),
            _resident((None, 1, D_CONV), lay),
            _resident((None, 1, D_CONV), lay),
            _resident((None, 1, d), lay),
            _resident((None, d, d), lay),
            _resident((None, 1, d), lay),
            _resident((None, 1, d), lay),
            _resident((None, d, D_FF), lay),
            _resident((None, d, D_FF), lay),
            _resident((None, D_FF, d), lay),
        ],
        out_specs=pl.BlockSpec((1, tm, d), tok),
        out_shape=jax.ShapeDtypeStruct((bsz, seq, d), F32),
        compiler_params=_params(("arbitrary", "arbitrary")),
        name="mix_out",
    )(x, bg, u, u, u, fn, mod_all, conv_w, conv_b, g_conv, g_post_mix, w_out, g_pre_ffn,
      g_post_ffn, w_gate, w_up, w_down)


def kernel(x_prompt, x_sample, c_prompt, c_sample, w_ada, b_ada, g_pre_mix, g_post_mix, w_in,
           conv_w, conv_b, g_conv, g_fft, w_out, g_pre_ffn, g_post_ffn, w_gate, w_up, w_down):
    depth = w_in.shape[0]
    row3 = lambda g: g.reshape(depth, 1, g.shape[-1])

    w_fold = _fold_channel_dft(w_in[:, :, 3 * D_CONV:])
    w_in_ext = jnp.concatenate([w_in[:, :, :3 * D_CONV], w_fold], axis=-1).astype(BF16)
    w_out_b = w_out.astype(BF16)
    w_gate_b = w_gate.astype(BF16)
    w_up_b = w_up.astype(BF16)
    w_down_b = w_down.astype(BF16)

    c_all = jnp.concatenate([c_prompt, c_sample], axis=0)
    mod_all = _modulation(c_all, w_ada, b_ada)
    mod_all = mod_all.reshape(depth, c_all.shape[0], 1, N_MOD * D_MODEL)

    g_pre_mix3, g_post_mix3 = row3(g_pre_mix), row3(g_post_mix)
    g_conv3, g_fft3, conv_b3 = row3(g_conv), row3(g_fft), row3(conv_b)
    g_pre_ffn3, g_post_ffn3 = row3(g_pre_ffn), row3(g_post_ffn)

    outs = []
    for x, row0 in ((x_prompt, 0), (x_sample, c_prompt.shape[0])):
        tables = _fft_tables(x.shape[1])
        for layer in range(depth):
            bg, u, a = _mix_in(x, mod_all, g_pre_mix3, w_in_ext, layer, row0)
            fn = _seq_fft(tables, a, g_fft3, layer)
            x = _mix_out(x, bg, u, fn, mod_all, conv_w, conv_b3, g_conv3, g_post_mix3, w_out_b,
                         g_pre_ffn3, g_post_ffn3, w_gate_b, w_up_b, w_down_b, layer, row0)
        outs.append(x)
    return tuple(outs)
```

```python
import functools
import math

import jax
import jax.numpy as jnp
from jax import lax
from jax.experimental import pallas as pl
from jax.experimental.pallas import tpu as pltpu

D_MODEL = 1024
D_CONV = 512
D_FFT = 512
HEAD_DIM = 64
N_FFT_GROUPS = D_FFT // HEAD_DIM
D_FF = 2816
N_MOD = 6
EPS = 1e-6
IN_EXT = 3 * D_CONV + 2 * D_FFT

F32 = jnp.float32
BF16 = jnp.bfloat16

V7X_VMEM_BYTES = 64 * 1024 * 1024
VMEM_LIMIT = V7X_VMEM_BYTES - 8 * 1024 * 1024

MIX_IN_TILE = 512
MIX_OUT_TILE = 1024
SUB_TILES = 4
BF16_SUBLANES = 16
FFT_N1 = BF16_SUBLANES
FFT_ROWS = BF16_SUBLANES
FF_CHUNK = 768
CONV_HALO = BF16_SUBLANES


def _rms(v, g):
    return v * lax.rsqrt(jnp.mean(v * v, axis=-1, keepdims=True) + EPS) * g


def _resident(block_shape, index_map):
    return pl.BlockSpec(block_shape, index_map, pipeline_mode=pl.Buffered(1))


def _params(sem):
    return pltpu.CompilerParams(dimension_semantics=sem, vmem_limit_bytes=VMEM_LIMIT)


def _fft_tables_body(ka_ref, db_ref, tw_ref, perm_ref, *, n1, n2):
    half = n1 * FFT_ROWS
    i = lax.broadcasted_iota(jnp.int32, (2 * half, 2 * half), 0)
    j = lax.broadcasted_iota(jnp.int32, (2 * half, 2 * half), 1)
    k1 = (i & (half - 1)) // FFT_ROWS
    s1 = (j & (half - 1)) // FFT_ROWS
    same_r = (i & (FFT_ROWS - 1)) == (j & (FFT_ROWS - 1))
    ang = ((k1 * s1) & (n1 - 1)).astype(F32) * (2.0 * math.pi / n1)
    pi_, pj = i >= half, j >= half
    val = jnp.where(pi_ == pj, jnp.cos(ang), jnp.where(pj, -jnp.sin(ang), jnp.sin(ang)))
    ka_ref[...] = jnp.where(same_r, val, 0.0).astype(BF16)

    k2 = lax.broadcasted_iota(jnp.int32, (n2, 2 * n2), 0)
    q = lax.broadcasted_iota(jnp.int32, (n2, 2 * n2), 1)
    ang = ((k2 * (q & (n2 - 1))) & (n2 - 1)).astype(F32) * (2.0 * math.pi / n2)
    db_ref[...] = jnp.where(q < n2, jnp.cos(ang), -jnp.sin(ang)).astype(BF16)

    t = lax.broadcasted_iota(jnp.int32, (n1 * n2, 128), 0)
    ang = ((t // n2) * (t & (n2 - 1))).astype(F32) * (2.0 * math.pi / (n1 * n2))
    tw_ref[0] = jnp.cos(ang)
    tw_ref[1] = jnp.sin(ang)

    ts = perm_ref.shape[0]
    tok = lax.broadcasted_iota(jnp.int32, (ts, ts), 0)
    src = lax.broadcasted_iota(jnp.int32, (ts, ts), 1)
    perm_ref[...] = (src == (tok & (n1 - 1)) * (ts // n1) + tok // n1).astype(BF16)


def _fft_tables(seq):
    n1, n2 = FFT_N1, seq // FFT_N1
    half = n1 * FFT_ROWS
    ts = MIX_OUT_TILE // SUB_TILES
    return pl.pallas_call(
        functools.partial(_fft_tables_body, n1=n1, n2=n2),
        out_shape=[
            jax.ShapeDtypeStruct((2 * half, 2 * half), BF16),
            jax.ShapeDtypeStruct((n2, 2 * n2), BF16),
            jax.ShapeDtypeStruct((2, seq, 128), F32),
            jax.ShapeDtypeStruct((ts, ts), BF16),
        ],
        compiler_params=pltpu.CompilerParams(vmem_limit_bytes=VMEM_LIMIT),
        name=f"fft_tables_{seq}",
    )()


def _fold_body(w_ref, o_ref):
    c = lax.broadcasted_iota(jnp.int32, (HEAD_DIM, 2 * HEAD_DIM), 0)
    m = lax.broadcasted_iota(jnp.int32, (HEAD_DIM, 2 * HEAD_DIM), 1)
    ang = ((c * (m & (HEAD_DIM - 1))) & (HEAD_DIM - 1)).astype(F32) * (2.0 * math.pi / HEAD_DIM)
    tab = jnp.where(m < HEAD_DIM, jnp.cos(ang), jnp.sin(ang))
    o_ref[...] = jnp.dot(w_ref[...], tab, precision=lax.Precision.HIGHEST,
                         preferred_element_type=F32)


def _fold_channel_dft(w_f):
    depth = w_f.shape[0]
    rows = depth * D_MODEL * N_FFT_GROUPS
    tile = 4096
    flat = w_f.reshape(rows, HEAD_DIM)
    out = pl.pallas_call(
        _fold_body,
        grid=(rows // tile,),
        in_specs=[pl.BlockSpec((tile, HEAD_DIM), lambda i: (i, 0))],
        out_specs=pl.BlockSpec((tile, 2 * HEAD_DIM), lambda i: (i, 0)),
        out_shape=jax.ShapeDtypeStruct((rows, 2 * HEAD_DIM), F32),
        compiler_params=_params(("arbitrary",)),
        name="fold_channel_dft",
    )(flat)
    out = out.reshape(depth, D_MODEL, N_FFT_GROUPS, 2, HEAD_DIM)
    return out.transpose(0, 1, 3, 2, 4).reshape(depth, D_MODEL, 2 * D_FFT)


def _mod_body(c_ref, w_ref, b_ref, o_ref):
    c = c_ref[...]
    act = c * jax.nn.sigmoid(c)
    o_ref[0] = jnp.dot(act, w_ref[0], precision=lax.Precision.HIGHEST,
                       preferred_element_type=F32) + b_ref[0]


def _modulation(c_all, w_ada, b_ada):
    depth = w_ada.shape[0]
    rows = c_all.shape[0]
    return pl.pallas_call(
        _mod_body,
        grid=(depth, N_MOD),
        in_specs=[
            pl.BlockSpec((rows, D_MODEL), lambda l, j: (0, 0)),
            pl.BlockSpec((1, D_MODEL, D_MODEL), lambda l, j: (l, 0, j)),
            pl.BlockSpec((1, 1, D_MODEL), lambda l, j: (l, 0, j)),
        ],
        out_specs=pl.BlockSpec((1, rows, D_MODEL), lambda l, j: (l, 0, j)),
        out_shape=jax.ShapeDtypeStruct((depth, rows, N_MOD * D_MODEL), F32),
        compiler_params=_params(("arbitrary", "arbitrary")),
        name="adaln_modulation",
    )(c_all, w_ada, b_ada.reshape(depth, 1, N_MOD * D_MODEL))


def _mix_in_body(x_ref, mod_ref, g_ref, w_ref, bg_ref, u_ref, a_ref):
    d = D_MODEL
    sh = mod_ref[:, 0 * d:1 * d]
    sc = mod_ref[:, 1 * d:2 * d]
    h = _rms(x_ref[0], g_ref[...]) * (1.0 + sc) + sh
    z = jnp.dot(h.astype(BF16), w_ref[...], preferred_element_type=F32)
    bg_ref[0] = z[:, :D_CONV].astype(BF16)
    u_ref[0] = (z[:, D_CONV:2 * D_CONV] * z[:, 2 * D_CONV:3 * D_CONV]).astype(BF16)
    a_ref[0, 0] = z[:, 3 * D_CONV:3 * D_CONV + D_FFT].astype(BF16)
    a_ref[0, 1] = z[:, 3 * D_CONV + D_FFT:].astype(BF16)


def _mix_in(x, mod_all, g_pre, w_in_ext, layer, row0):
    bsz, seq, d = x.shape
    tm = MIX_IN_TILE
    return pl.pallas_call(
        _mix_in_body,
        grid=(bsz, seq // tm),
        in_specs=[
            pl.BlockSpec((1, tm, d), lambda b, i: (b, i, 0)),
            pl.BlockSpec((None, None, 1, N_MOD * d), lambda b, i: (layer, row0 + b, 0, 0)),
            _resident((None, 1, d), lambda b, i: (layer, 0, 0)),
            _resident((None, d, IN_EXT), lambda b, i: (layer, 0, 0)),
        ],
        out_specs=[
            pl.BlockSpec((1, tm, D_CONV), lambda b, i: (b, i, 0)),
            pl.BlockSpec((1, tm, D_CONV), lambda b, i: (b, i, 0)),
            pl.BlockSpec((1, 2, tm, D_FFT), lambda b, i: (b, 0, i, 0)),
        ],
        out_shape=[
            jax.ShapeDtypeStruct((bsz, seq, D_CONV), BF16),
            jax.ShapeDtypeStruct((bsz, seq, D_CONV), BF16),
            jax.ShapeDtypeStruct((bsz, 2, seq, D_FFT), BF16),
        ],
        compiler_params=_params(("arbitrary", "arbitrary")),
        name="mix_in",
    )(x, mod_all, g_pre, w_in_ext)


def _seq_fft_body(a_ref, ka_ref, db_ref, tw_ref, g_ref, o_ref, y_ref, *, n1, n2):
    rows = FFT_ROWS
    half = n1 * rows
    lanes = tw_ref.shape[-1]
    for j in range(n2 // rows):
        r0 = j * rows
        x = jnp.concatenate([a_ref[0, 0, :, r0:r0 + rows, :].reshape(half, D_FFT),
                             a_ref[0, 1, :, r0:r0 + rows, :].reshape(half, D_FFT)], axis=0)
        y = jnp.dot(ka_ref[...], x, preferred_element_type=F32)
        cw = tw_ref[0, :, r0:r0 + rows, :].reshape(half, lanes)
        sw = tw_ref[1, :, r0:r0 + rows, :].reshape(half, lanes)
        yr_cols, yi_cols = [], []
        for c0 in range(0, D_FFT, lanes):
            yr, yi = y[:half, c0:c0 + lanes], y[half:, c0:c0 + lanes]
            yr_cols.append(yr * cw - yi * sw)
            yi_cols.append(yr * sw + yi * cw)
        y_ref[:, 0, r0:r0 + rows, :] = (
            jnp.concatenate(yr_cols, axis=1).astype(BF16).reshape(n1, rows, D_FFT))
        y_ref[:, 1, r0:r0 + rows, :] = (
            jnp.concatenate(yi_cols, axis=1).astype(BF16).reshape(n1, rows, D_FFT))

    for k1 in range(n1):
        f = jnp.dot(db_ref[...], y_ref[k1].reshape(2 * n2, D_FFT), preferred_element_type=F32)
        o_ref[0, k1] = _rms(f, g_ref[...]).astype(BF16)


def _seq_fft(tables, a, g_fft, layer):
    ka, db, tw, _ = tables
    bsz, _, seq, _ = a.shape
    n1, n2 = FFT_N1, seq // FFT_N1
    a5 = a.reshape(bsz, 2, n1, n2, D_FFT)
    tw4 = tw.reshape(2, n1, n2, tw.shape[-1])
    return pl.pallas_call(
        functools.partial(_seq_fft_body, n1=n1, n2=n2),
        grid=(bsz,),
        in_specs=[
            pl.BlockSpec((1, 2, n1, n2, D_FFT), lambda b: (b, 0, 0, 0, 0)),
            _resident(ka.shape, lambda b: (0, 0)),
            _resident(db.shape, lambda b: (0, 0)),
            _resident(tw4.shape, lambda b: (0, 0, 0, 0)),
            _resident((None, 1, D_FFT), lambda b: (layer, 0, 0)),
        ],
        out_specs=pl.BlockSpec((1, n1, n2, D_FFT), lambda b: (b, 0, 0, 0)),
        out_shape=jax.ShapeDtypeStruct((bsz, n1, n2, D_FFT), BF16),
        scratch_shapes=[pltpu.VMEM((n1, 2, n2, D_FFT), BF16)],
        compiler_params=_params(("arbitrary",)),
        name="seq_fft",
    )(a5, ka, db, tw4, g_fft)


def _mix_out_body(x_ref, bg_ref, u_ref, up_ref, un_ref, fn_ref, perm_ref, mod_ref, cw_ref, cb_ref,
                  gc_ref, gpm_ref, wo_ref, gpf_ref, gpo_ref, wg_ref, wu_ref, wd_ref, o_ref):
    d = D_MODEL
    tm = u_ref.shape[1]
    ts = tm // SUB_TILES
    i = pl.program_id(1)
    last = pl.num_programs(1) - 1

    def rows_of(sub):
        return slice(sub * ts, (sub + 1) * ts)

    def conv_branch(sub):
        r0 = sub * ts
        u = u_ref[0, rows_of(sub), :].astype(F32)
        if sub == 0:
            prev = jnp.where(i > 0, up_ref[0, CONV_HALO - 1:CONV_HALO, :].astype(F32), 0.0)
        else:
            prev = u_ref[0, r0 - 1:r0, :].astype(F32)
        if sub == SUB_TILES - 1:
            nxt = jnp.where(i < last, un_ref[0, 0:1, :].astype(F32), 0.0)
        else:
            nxt = u_ref[0, r0 + ts:r0 + ts + 1, :].astype(F32)
        row = lax.broadcasted_iota(jnp.int32, u.shape, 0)
        u_m1 = jnp.where(row == 0, prev, pltpu.roll(u, 1, 0))
        u_p1 = jnp.where(row == ts - 1, nxt, pltpu.roll(u, ts - 1, 0))
        conv = u_m1 * cw_ref[0:1, :] + u * cw_ref[1:2, :] + u_p1 * cw_ref[2:3, :] + cb_ref[...]
        conv_n = _rms(bg_ref[0, rows_of(sub), :] * conv, gc_ref[...]).astype(BF16)
        q = ts // FFT_N1
        fn = fn_ref[0, :, sub * q:(sub + 1) * q, :].reshape(ts, D_FFT)
        fn = jnp.dot(perm_ref[...], fn, preferred_element_type=F32).astype(BF16)
        return jnp.concatenate([conv_n, fn], axis=-1)

    def out_proj(merged):
        return jnp.dot(merged, wo_ref[...], preferred_element_type=F32)

    def residual_and_ffn_input(sub, o):
        x1 = x_ref[0, rows_of(sub), :] + mod_ref[:, 2 * d:3 * d] * _rms(o, gpm_ref[...])
        h = _rms(x1, gpf_ref[...]) * (1.0 + mod_ref[:, 4 * d:5 * d]) + mod_ref[:, 3 * d:4 * d]
        return x1, h.astype(BF16)

    def ffn_chunk(hb, ff, c0):
        c1 = min(c0 + FF_CHUNK, D_FF)
        gate = jnp.dot(hb, wg_ref[:, c0:c1], preferred_element_type=F32)
        up = jnp.dot(hb, wu_ref[:, c0:c1], preferred_element_type=F32)
        act = (gate * jax.nn.sigmoid(gate) * up).astype(BF16)
        part = jnp.dot(act, wd_ref[c0:c1, :], preferred_element_type=F32)
        return part if ff is None else ff + part

    def finish(sub, x1, ff):
        o_ref[0, rows_of(sub), :] = x1 + mod_ref[:, 5 * d:6 * d] * _rms(ff, gpo_ref[...])

    chunks = list(range(0, D_FF, FF_CHUNK))
    x1, hb = residual_and_ffn_input(0, out_proj(conv_branch(0)))
    pending = None
    for sub in range(SUB_TILES):
        more = sub + 1 < SUB_TILES
        if more:
            o_next = out_proj(conv_branch(sub + 1))
        ff = ffn_chunk(hb, None, chunks[0])
        if pending is not None:
            finish(*pending)
        if more:
            x1_next, hb_next = residual_and_ffn_input(sub + 1, o_next)
        for c0 in chunks[1:]:
            ff = ffn_chunk(hb, ff, c0)
        pending = (sub, x1, ff)
        if more:
            x1, hb = x1_next, hb_next
    finish(*pending)


def _mix_out(x, bg, u, fn, perm, mod_all, conv_w, conv_b, g_conv, g_post_mix, w_out, g_pre_ffn,
             g_post_ffn, w_gate, w_up, w_down, layer, row0):
    bsz, seq, d = x.shape
    tm = MIX_OUT_TILE
    halo_per_tile = tm // CONV_HALO
    n_halo = seq // CONV_HALO
    tok = lambda b, i: (b, i, 0)
    lay = lambda b, i: (layer, 0, 0)
    return pl.pallas_call(
        _mix_out_body,
        grid=(bsz, seq // tm),
        in_specs=[
            pl.BlockSpec((1, tm, d), tok),
            pl.BlockSpec((1, tm, D_CONV), tok),
            pl.BlockSpec((1, tm, D_CONV), tok),
            pl.BlockSpec((1, CONV_HALO, D_CONV),
                         lambda b, i: (b, jnp.maximum(i * halo_per_tile - 1, 0), 0)),
            pl.BlockSpec((1, CONV_HALO, D_CONV),
                         lambda b, i: (b, jnp.minimum((i + 1) * halo_per_tile, n_halo - 1), 0)),
            pl.BlockSpec((1, FFT_N1, tm // FFT_N1, D_FFT), lambda b, i: (b, 0, i, 0)),
            _resident(perm.shape, lambda b, i: (0, 0)),
            pl.BlockSpec((None, None, 1, N_MOD * d), lambda b, i: (layer, row0 + b, 0, 0)),
            _resident((None, 3, D_CONV), lay),
            _resident((None, 1, D_CONV), lay),
            _resident((None, 1, D_CONV), lay),
            _resident((None, 1, d), lay),
            _resident((None, d, d), lay),
            _resident((None, 1, d), lay),
            _resident((None, 1, d), lay),
            _resident((None, d, D_FF), lay),
            _resident((None, d, D_FF), lay),
            _resident((None, D_FF, d), lay),
        ],
        out_specs=pl.BlockSpec((1, tm, d), tok),
        out_shape=jax.ShapeDtypeStruct((bsz, seq, d), F32),
        compiler_params=_params(("arbitrary", "arbitrary")),
        name="mix_out",
    )(x, bg, u, u, u, fn, perm, mod_all, conv_w, conv_b, g_conv, g_post_mix, w_out, g_pre_ffn,
      g_post_ffn, w_gate, w_up, w_down)


def kernel(x_prompt, x_sample, c_prompt, c_sample, w_ada, b_ada, g_pre_mix, g_post_mix, w_in,
           conv_w, conv_b, g_conv, g_fft, w_out, g_pre_ffn, g_post_ffn, w_gate, w_up, w_down):
    depth = w_in.shape[0]
    row3 = lambda g: g.reshape(depth, 1, g.shape[-1])

    w_fold = _fold_channel_dft(w_in[:, :, 3 * D_CONV:])
    w_in_ext = jnp.concatenate([w_in[:, :, :3 * D_CONV], w_fold], axis=-1).astype(BF16)
    w_out_b = w_out.astype(BF16)
    w_gate_b = w_gate.astype(BF16)
    w_up_b = w_up.astype(BF16)
    w_down_b = w_down.astype(BF16)

    c_all = jnp.concatenate([c_prompt, c_sample], axis=0)
    mod_all = _modulation(c_all, w_ada, b_ada)
    mod_all = mod_all.reshape(depth, c_all.shape[0], 1, N_MOD * D_MODEL)

    g_pre_mix3, g_post_mix3 = row3(g_pre_mix), row3(g_post_mix)
    g_conv3, g_fft3, conv_b3 = row3(g_conv), row3(g_fft), row3(conv_b)
    g_pre_ffn3, g_post_ffn3 = row3(g_pre_ffn), row3(g_post_ffn)

    outs = []
    for x, row0 in ((x_prompt, 0), (x_sample, c_prompt.shape[0])):
        tables = _fft_tables(x.shape[1])
        for layer in range(depth):
            bg, u, a = _mix_in(x, mod_all, g_pre_mix3, w_in_ext, layer, row0)
            fn = _seq_fft(tables, a, g_fft3, layer)
            x = _mix_out(x, bg, u, fn, tables[3], mod_all, conv_w, conv_b3, g_conv3, g_post_mix3, w_out_b,
                         g_pre_ffn3, g_post_ffn3, w_gate_b, w_up_b, w_down_b, layer, row0)
        outs.append(x)
    return tuple(outs)
```

```python
import functools
import math

import jax
import jax.numpy as jnp
from jax import lax
from jax.experimental import pallas as pl
from jax.experimental.pallas import tpu as pltpu

D_MODEL = 1024
D_CONV = 512
D_FFT = 512
HEAD_DIM = 64
N_FFT_GROUPS = D_FFT // HEAD_DIM
D_FF = 2816
N_MOD = 6
EPS = 1e-6
IN_EXT = 3 * D_CONV + 2 * D_FFT

F32 = jnp.float32
BF16 = jnp.bfloat16

V7X_VMEM_BYTES = 64 * 1024 * 1024
VMEM_LIMIT = V7X_VMEM_BYTES - 8 * 1024 * 1024

MIX_IN_TILE = 512
MIX_OUT_TILE = 1024
SUB_TILES = 4
BF16_SUBLANES = 16
FFT_N1 = BF16_SUBLANES
FFT_ROWS = BF16_SUBLANES
FF_CHUNK = 768
CONV_HALO = BF16_SUBLANES


def _rms(v, g):
    return v * lax.rsqrt(jnp.mean(v * v, axis=-1, keepdims=True) + EPS) * g


def _resident(block_shape, index_map):
    return pl.BlockSpec(block_shape, index_map, pipeline_mode=pl.Buffered(1))


def _params(sem):
    return pltpu.CompilerParams(dimension_semantics=sem, vmem_limit_bytes=VMEM_LIMIT)


def _fft_tables_body(ka_ref, db_ref, tw_ref, perm_ref, *, n1, n2):
    half = n1 * FFT_ROWS
    i = lax.broadcasted_iota(jnp.int32, (2 * half, 2 * half), 0)
    j = lax.broadcasted_iota(jnp.int32, (2 * half, 2 * half), 1)
    k1 = (i & (half - 1)) // FFT_ROWS
    s1 = (j & (half - 1)) // FFT_ROWS
    same_r = (i & (FFT_ROWS - 1)) == (j & (FFT_ROWS - 1))
    ang = ((k1 * s1) & (n1 - 1)).astype(F32) * (2.0 * math.pi / n1)
    pi_, pj = i >= half, j >= half
    val = jnp.where(pi_ == pj, jnp.cos(ang), jnp.where(pj, -jnp.sin(ang), jnp.sin(ang)))
    ka_ref[...] = jnp.where(same_r, val, 0.0).astype(BF16)

    k2 = lax.broadcasted_iota(jnp.int32, (n2, 2 * n2), 0)
    q = lax.broadcasted_iota(jnp.int32, (n2, 2 * n2), 1)
    ang = ((k2 * (q & (n2 - 1))) & (n2 - 1)).astype(F32) * (2.0 * math.pi / n2)
    db_ref[...] = jnp.where(q < n2, jnp.cos(ang), -jnp.sin(ang)).astype(BF16)

    t = lax.broadcasted_iota(jnp.int32, (n1 * n2, 128), 0)
    ang = ((t // n2) * (t & (n2 - 1))).astype(F32) * (2.0 * math.pi / (n1 * n2))
    tw_ref[0] = jnp.cos(ang)
    tw_ref[1] = jnp.sin(ang)

    ts = perm_ref.shape[0]
    tok = lax.broadcasted_iota(jnp.int32, (ts, ts), 0)
    src = lax.broadcasted_iota(jnp.int32, (ts, ts), 1)
    perm_ref[...] = (src == (tok & (n1 - 1)) * (ts // n1) + tok // n1).astype(BF16)


def _fft_tables(seq):
    n1, n2 = FFT_N1, seq // FFT_N1
    half = n1 * FFT_ROWS
    ts = MIX_OUT_TILE // SUB_TILES
    return pl.pallas_call(
        functools.partial(_fft_tables_body, n1=n1, n2=n2),
        out_shape=[
            jax.ShapeDtypeStruct((2 * half, 2 * half), BF16),
            jax.ShapeDtypeStruct((n2, 2 * n2), BF16),
            jax.ShapeDtypeStruct((2, seq, 128), F32),
            jax.ShapeDtypeStruct((ts, ts), BF16),
        ],
        compiler_params=pltpu.CompilerParams(vmem_limit_bytes=VMEM_LIMIT),
        name=f"fft_tables_{seq}",
    )()


def _prep_w_in_body(w_ref, o_ref, tab_ref):
    @pl.when((pl.program_id(0) == 0) & (pl.program_id(1) == 0))
    def _():
        r = lax.broadcasted_iota(jnp.int32, (D_FFT, 2 * D_FFT), 0)
        q = lax.broadcasted_iota(jnp.int32, (D_FFT, 2 * D_FFT), 1)
        same_group = (r // HEAD_DIM) == ((q & (D_FFT - 1)) // HEAD_DIM)
        cm = ((r & (HEAD_DIM - 1)) * (q & (HEAD_DIM - 1))) & (HEAD_DIM - 1)
        ang = cm.astype(F32) * (2.0 * math.pi / HEAD_DIM)
        tab_ref[...] = jnp.where(same_group, jnp.where(q < D_FFT, jnp.cos(ang), jnp.sin(ang)), 0.0)

    o_ref[0, :, :3 * D_CONV] = w_ref[0, :, :3 * D_CONV].astype(BF16)
    o_ref[0, :, 3 * D_CONV:] = jnp.dot(
        w_ref[0, :, 3 * D_CONV:], tab_ref[...], precision=lax.Precision.HIGHEST,
        preferred_element_type=F32).astype(BF16)


def _prep_w_in(w_in):
    depth = w_in.shape[0]
    rows = 256
    return pl.pallas_call(
        _prep_w_in_body,
        grid=(depth, D_MODEL // rows),
        in_specs=[pl.BlockSpec((1, rows, w_in.shape[-1]), lambda l, i: (l, i, 0))],
        out_specs=pl.BlockSpec((1, rows, IN_EXT), lambda l, i: (l, i, 0)),
        out_shape=jax.ShapeDtypeStruct((depth, D_MODEL, IN_EXT), BF16),
        scratch_shapes=[pltpu.VMEM((D_FFT, 2 * D_FFT), F32)],
        compiler_params=_params(("arbitrary", "arbitrary")),
        name="prep_w_in",
    )(w_in)


def _mod_body(c_ref, w_ref, b_ref, o_ref):
    c = c_ref[...]
    act = c * jax.nn.sigmoid(c)
    o_ref[0] = jnp.dot(act, w_ref[0], precision=lax.Precision.HIGHEST,
                       preferred_element_type=F32) + b_ref[0]


def _modulation(c_all, w_ada, b_ada):
    depth = w_ada.shape[0]
    rows = c_all.shape[0]
    return pl.pallas_call(
        _mod_body,
        grid=(depth, N_MOD),
        in_specs=[
            pl.BlockSpec((rows, D_MODEL), lambda l, j: (0, 0)),
            pl.BlockSpec((1, D_MODEL, D_MODEL), lambda l, j: (l, 0, j)),
            pl.BlockSpec((1, 1, D_MODEL), lambda l, j: (l, 0, j)),
        ],
        out_specs=pl.BlockSpec((1, rows, D_MODEL), lambda l, j: (l, 0, j)),
        out_shape=jax.ShapeDtypeStruct((depth, rows, N_MOD * D_MODEL), F32),
        compiler_params=_params(("arbitrary", "arbitrary")),
        name="adaln_modulation",
    )(c_all, w_ada, b_ada.reshape(depth, 1, N_MOD * D_MODEL))


def _mix_in_body(x_ref, mod_ref, g_ref, w_ref, bg_ref, u_ref, a_ref):
    d = D_MODEL
    sh = mod_ref[:, 0 * d:1 * d]
    sc = mod_ref[:, 1 * d:2 * d]
    h = _rms(x_ref[0], g_ref[...]) * (1.0 + sc) + sh
    z = jnp.dot(h.astype(BF16), w_ref[...], preferred_element_type=F32)
    bg_ref[0] = z[:, :D_CONV].astype(BF16)
    u_ref[0] = (z[:, D_CONV:2 * D_CONV] * z[:, 2 * D_CONV:3 * D_CONV]).astype(BF16)
    a_ref[0, 0] = z[:, 3 * D_CONV:3 * D_CONV + D_FFT].astype(BF16)
    a_ref[0, 1] = z[:, 3 * D_CONV + D_FFT:].astype(BF16)


def _mix_in(x, mod_all, g_pre, w_in_ext, layer, row0):
    bsz, seq, d = x.shape
    tm = MIX_IN_TILE
    return pl.pallas_call(
        _mix_in_body,
        grid=(bsz, seq // tm),
        in_specs=[
            pl.BlockSpec((1, tm, d), lambda b, i: (b, i, 0)),
            pl.BlockSpec((None, None, 1, N_MOD * d), lambda b, i: (layer, row0 + b, 0, 0)),
            _resident((None, 1, d), lambda b, i: (layer, 0, 0)),
            _resident((None, d, IN_EXT), lambda b, i: (layer, 0, 0)),
        ],
        out_specs=[
            pl.BlockSpec((1, tm, D_CONV), lambda b, i: (b, i, 0)),
            pl.BlockSpec((1, tm, D_CONV), lambda b, i: (b, i, 0)),
            pl.BlockSpec((1, 2, tm, D_FFT), lambda b, i: (b, 0, i, 0)),
        ],
        out_shape=[
            jax.ShapeDtypeStruct((bsz, seq, D_CONV), BF16),
            jax.ShapeDtypeStruct((bsz, seq, D_CONV), BF16),
            jax.ShapeDtypeStruct((bsz, 2, seq, D_FFT), BF16),
        ],
        compiler_params=_params(("arbitrary", "arbitrary")),
        name="mix_in",
    )(x, mod_all, g_pre, w_in_ext)


def _seq_fft_body(a_ref, ka_ref, db_ref, tw_ref, g_ref, o_ref, y_ref, *, n1, n2):
    rows = FFT_ROWS
    half = n1 * rows
    lanes = tw_ref.shape[-1]
    for j in range(n2 // rows):
        r0 = j * rows
        x = jnp.concatenate([a_ref[0, 0, :, r0:r0 + rows, :].reshape(half, D_FFT),
                             a_ref[0, 1, :, r0:r0 + rows, :].reshape(half, D_FFT)], axis=0)
        y = jnp.dot(ka_ref[...], x, preferred_element_type=F32)
        cw = tw_ref[0, :, r0:r0 + rows, :].reshape(half, lanes)
        sw = tw_ref[1, :, r0:r0 + rows, :].reshape(half, lanes)
        yr_cols, yi_cols = [], []
        for c0 in range(0, D_FFT, lanes):
            yr, yi = y[:half, c0:c0 + lanes], y[half:, c0:c0 + lanes]
            yr_cols.append(yr * cw - yi * sw)
            yi_cols.append(yr * sw + yi * cw)
        y_ref[:, 0, r0:r0 + rows, :] = (
            jnp.concatenate(yr_cols, axis=1).astype(BF16).reshape(n1, rows, D_FFT))
        y_ref[:, 1, r0:r0 + rows, :] = (
            jnp.concatenate(yi_cols, axis=1).astype(BF16).reshape(n1, rows, D_FFT))

    for k1 in range(n1):
        f = jnp.dot(db_ref[...], y_ref[k1].reshape(2 * n2, D_FFT), preferred_element_type=F32)
        o_ref[0, k1] = _rms(f, g_ref[...]).astype(BF16)


def _seq_fft(tables, a, g_fft, layer):
    ka, db, tw, _ = tables
    bsz, _, seq, _ = a.shape
    n1, n2 = FFT_N1, seq // FFT_N1
    a5 = a.reshape(bsz, 2, n1, n2, D_FFT)
    tw4 = tw.reshape(2, n1, n2, tw.shape[-1])
    return pl.pallas_call(
        functools.partial(_seq_fft_body, n1=n1, n2=n2),
        grid=(bsz,),
        in_specs=[
            pl.BlockSpec((1, 2, n1, n2, D_FFT), lambda b: (b, 0, 0, 0, 0)),
            _resident(ka.shape, lambda b: (0, 0)),
            _resident(db.shape, lambda b: (0, 0)),
            _resident(tw4.shape, lambda b: (0, 0, 0, 0)),
            _resident((None, 1, D_FFT), lambda b: (layer, 0, 0)),
        ],
        out_specs=pl.BlockSpec((1, n1, n2, D_FFT), lambda b: (b, 0, 0, 0)),
        out_shape=jax.ShapeDtypeStruct((bsz, n1, n2, D_FFT), BF16),
        scratch_shapes=[pltpu.VMEM((n1, 2, n2, D_FFT), BF16)],
        compiler_params=_params(("arbitrary",)),
        name="seq_fft",
    )(a5, ka, db, tw4, g_fft)


def _mix_out_body(x_ref, bg_ref, u_ref, up_ref, un_ref, fn_ref, perm_ref, mod_ref, cw_ref, cb_ref,
                  gc_ref, gpm_ref, wo_ref, gpf_ref, gpo_ref, wg_ref, wu_ref, wd_ref, o_ref):
    d = D_MODEL
    tm = u_ref.shape[1]
    ts = tm // SUB_TILES
    i = pl.program_id(1)
    last = pl.num_programs(1) - 1

    def rows_of(sub):
        return slice(sub * ts, (sub + 1) * ts)

    def conv_branch(sub):
        r0 = sub * ts
        u = u_ref[0, rows_of(sub), :].astype(F32)
        if sub == 0:
            prev = jnp.where(i > 0, up_ref[0, CONV_HALO - 1:CONV_HALO, :].astype(F32), 0.0)
        else:
            prev = u_ref[0, r0 - 1:r0, :].astype(F32)
        if sub == SUB_TILES - 1:
            nxt = jnp.where(i < last, un_ref[0, 0:1, :].astype(F32), 0.0)
        else:
            nxt = u_ref[0, r0 + ts:r0 + ts + 1, :].astype(F32)
        row = lax.broadcasted_iota(jnp.int32, u.shape, 0)
        u_m1 = jnp.where(row == 0, prev, pltpu.roll(u, 1, 0))
        u_p1 = jnp.where(row == ts - 1, nxt, pltpu.roll(u, ts - 1, 0))
        conv = u_m1 * cw_ref[0:1, :] + u * cw_ref[1:2, :] + u_p1 * cw_ref[2:3, :] + cb_ref[...]
        conv_n = _rms(bg_ref[0, rows_of(sub), :] * conv, gc_ref[...]).astype(BF16)
        q = ts // FFT_N1
        fn = fn_ref[0, :, sub * q:(sub + 1) * q, :].reshape(ts, D_FFT)
        fn = jnp.dot(perm_ref[...], fn, preferred_element_type=F32).astype(BF16)
        return jnp.concatenate([conv_n, fn], axis=-1)

    def out_proj(merged):
        return jnp.dot(merged, wo_ref[...], preferred_element_type=F32)

    def residual_and_ffn_input(sub, o):
        x1 = x_ref[0, rows_of(sub), :] + mod_ref[:, 2 * d:3 * d] * _rms(o, gpm_ref[...])
        h = _rms(x1, gpf_ref[...]) * (1.0 + mod_ref[:, 4 * d:5 * d]) + mod_ref[:, 3 * d:4 * d]
        return x1, h.astype(BF16)

    def ffn_chunk(hb, ff, c0):
        c1 = min(c0 + FF_CHUNK, D_FF)
        gate = jnp.dot(hb, wg_ref[:, c0:c1], preferred_element_type=F32)
        up = jnp.dot(hb, wu_ref[:, c0:c1], preferred_element_type=F32)
        act = (gate * jax.nn.sigmoid(gate) * up).astype(BF16)
        part = jnp.dot(act, wd_ref[c0:c1, :], preferred_element_type=F32)
        return part if ff is None else ff + part

    def finish(sub, x1, ff):
        o_ref[0, rows_of(sub), :] = x1 + mod_ref[:, 5 * d:6 * d] * _rms(ff, gpo_ref[...])

    chunks = list(range(0, D_FF, FF_CHUNK))
    x1, hb = residual_and_ffn_input(0, out_proj(conv_branch(0)))
    pending = None
    for sub in range(SUB_TILES):
        more = sub + 1 < SUB_TILES
        if more:
            o_next = out_proj(conv_branch(sub + 1))
        ff = ffn_chunk(hb, None, chunks[0])
        if pending is not None:
            finish(*pending)
        if more:
            x1_next, hb_next = residual_and_ffn_input(sub + 1, o_next)
        for c0 in chunks[1:]:
            ff = ffn_chunk(hb, ff, c0)
        pending = (sub, x1, ff)
        if more:
            x1, hb = x1_next, hb_next
    finish(*pending)


def _mix_out(x, bg, u, fn, perm, mod_all, conv_w, conv_b, g_conv, g_post_mix, w_out, g_pre_ffn,
             g_post_ffn, w_gate, w_up, w_down, layer, row0):
    bsz, seq, d = x.shape
    tm = MIX_OUT_TILE
    halo_per_tile = tm // CONV_HALO
    n_halo = seq // CONV_HALO
    tok = lambda b, i: (b, i, 0)
    lay = lambda b, i: (layer, 0, 0)
    return pl.pallas_call(
        _mix_out_body,
        grid=(bsz, seq // tm),
        in_specs=[
            pl.BlockSpec((1, tm, d), tok),
            pl.BlockSpec((1, tm, D_CONV), tok),
            pl.BlockSpec((1, tm, D_CONV), tok),
            pl.BlockSpec((1, CONV_HALO, D_CONV),
                         lambda b, i: (b, jnp.maximum(i * halo_per_tile - 1, 0), 0)),
            pl.BlockSpec((1, CONV_HALO, D_CONV),
                         lambda b, i: (b, jnp.minimum((i + 1) * halo_per_tile, n_halo - 1), 0)),
            pl.BlockSpec((1, FFT_N1, tm // FFT_N1, D_FFT), lambda b, i: (b, 0, i, 0)),
            _resident(perm.shape, lambda b, i: (0, 0)),
            pl.BlockSpec((None, None, 1, N_MOD * d), lambda b, i: (layer, row0 + b, 0, 0)),
            _resident((None, 3, D_CONV), lay),
            _resident((None, 1, D_CONV), lay),
            _resident((None, 1, D_CONV), lay),
            _resident((None, 1, d), lay),
            _resident((None, d, d), lay),
            _resident((None, 1, d), lay),
            _resident((None, 1, d), lay),
            _resident((None, d, D_FF), lay),
            _resident((None, d, D_FF), lay),
            _resident((None, D_FF, d), lay),
        ],
        out_specs=pl.BlockSpec((1, tm, d), tok),
        out_shape=jax.ShapeDtypeStruct((bsz, seq, d), F32),
        compiler_params=_params(("arbitrary", "arbitrary")),
        name="mix_out",
    )(x, bg, u, u, u, fn, perm, mod_all, conv_w, conv_b, g_conv, g_post_mix, w_out, g_pre_ffn,
      g_post_ffn, w_gate, w_up, w_down)


def kernel(x_prompt, x_sample, c_prompt, c_sample, w_ada, b_ada, g_pre_mix, g_post_mix, w_in,
           conv_w, conv_b, g_conv, g_fft, w_out, g_pre_ffn, g_post_ffn, w_gate, w_up, w_down):
    depth = w_in.shape[0]
    row3 = lambda g: g.reshape(depth, 1, g.shape[-1])

    w_in_ext = _prep_w_in(w_in)
    w_out_b = w_out.astype(BF16)
    w_gate_b = w_gate.astype(BF16)
    w_up_b = w_up.astype(BF16)
    w_down_b = w_down.astype(BF16)

    c_all = jnp.concatenate([c_prompt, c_sample], axis=0)
    mod_all = _modulation(c_all, w_ada, b_ada)
    mod_all = mod_all.reshape(depth, c_all.shape[0], 1, N_MOD * D_MODEL)

    g_pre_mix3, g_post_mix3 = row3(g_pre_mix), row3(g_post_mix)
    g_conv3, g_fft3, conv_b3 = row3(g_conv), row3(g_fft), row3(conv_b)
    g_pre_ffn3, g_post_ffn3 = row3(g_pre_ffn), row3(g_post_ffn)

    outs = []
    for x, row0 in ((x_prompt, 0), (x_sample, c_prompt.shape[0])):
        tables = _fft_tables(x.shape[1])
        for layer in range(depth):
            bg, u, a = _mix_in(x, mod_all, g_pre_mix3, w_in_ext, layer, row0)
            fn = _seq_fft(tables, a, g_fft3, layer)
            x = _mix_out(x, bg, u, fn, tables[3], mod_all, conv_w, conv_b3, g_conv3, g_post_mix3, w_out_b,
                         g_pre_ffn3, g_post_ffn3, w_gate_b, w_up_b, w_down_b, layer, row0)
        outs.append(x)
    return tuple(outs)
```

```python
import functools
import math

import jax
import jax.numpy as jnp
from jax import lax
from jax.experimental import pallas as pl
from jax.experimental.pallas import tpu as pltpu

D_MODEL = 1024
D_CONV = 512
D_FFT = 512
HEAD_DIM = 64
N_FFT_GROUPS = D_FFT // HEAD_DIM
D_FF = 2816
N_MOD = 6
EPS = 1e-6
IN_EXT = 3 * D_CONV + 2 * D_FFT

F32 = jnp.float32
BF16 = jnp.bfloat16

V7X_VMEM_BYTES = 64 * 1024 * 1024
VMEM_LIMIT = V7X_VMEM_BYTES - 8 * 1024 * 1024

MIX_IN_TILE = 512
MIX_OUT_TILE = 1024
SUB_TILES = 4
BF16_SUBLANES = 16
FFT_N1 = BF16_SUBLANES
FFT_ROWS = BF16_SUBLANES
FF_CHUNK = 1536
CONV_HALO = BF16_SUBLANES


def _rms(v, g):
    return v * lax.rsqrt(jnp.mean(v * v, axis=-1, keepdims=True) + EPS) * g


def _resident(block_shape, index_map):
    return pl.BlockSpec(block_shape, index_map, pipeline_mode=pl.Buffered(1))


def _params(sem):
    return pltpu.CompilerParams(dimension_semantics=sem, vmem_limit_bytes=VMEM_LIMIT)


def _fft_tables_body(ka_ref, db_ref, tw_ref, perm_ref, *, n1, n2):
    half = n1 * FFT_ROWS
    i = lax.broadcasted_iota(jnp.int32, (2 * half, 2 * half), 0)
    j = lax.broadcasted_iota(jnp.int32, (2 * half, 2 * half), 1)
    k1 = (i & (half - 1)) // FFT_ROWS
    s1 = (j & (half - 1)) // FFT_ROWS
    same_r = (i & (FFT_ROWS - 1)) == (j & (FFT_ROWS - 1))
    ang = ((k1 * s1) & (n1 - 1)).astype(F32) * (2.0 * math.pi / n1)
    pi_, pj = i >= half, j >= half
    val = jnp.where(pi_ == pj, jnp.cos(ang), jnp.where(pj, -jnp.sin(ang), jnp.sin(ang)))
    ka_ref[...] = jnp.where(same_r, val, 0.0).astype(BF16)

    k2 = lax.broadcasted_iota(jnp.int32, (n2, 2 * n2), 0)
    q = lax.broadcasted_iota(jnp.int32, (n2, 2 * n2), 1)
    ang = ((k2 * (q & (n2 - 1))) & (n2 - 1)).astype(F32) * (2.0 * math.pi / n2)
    db_ref[...] = jnp.where(q < n2, jnp.cos(ang), -jnp.sin(ang)).astype(BF16)

    t = lax.broadcasted_iota(jnp.int32, (n1 * n2, 128), 0)
    ang = ((t // n2) * (t & (n2 - 1))).astype(F32) * (2.0 * math.pi / (n1 * n2))
    tw_ref[0] = jnp.cos(ang)
    tw_ref[1] = jnp.sin(ang)

    ts = perm_ref.shape[0]
    tok = lax.broadcasted_iota(jnp.int32, (ts, ts), 0)
    src = lax.broadcasted_iota(jnp.int32, (ts, ts), 1)
    perm_ref[...] = (src == (tok & (n1 - 1)) * (ts // n1) + tok // n1).astype(BF16)


def _fft_tables(seq):
    n1, n2 = FFT_N1, seq // FFT_N1
    half = n1 * FFT_ROWS
    ts = MIX_OUT_TILE // SUB_TILES
    return pl.pallas_call(
        functools.partial(_fft_tables_body, n1=n1, n2=n2),
        out_shape=[
            jax.ShapeDtypeStruct((2 * half, 2 * half), BF16),
            jax.ShapeDtypeStruct((n2, 2 * n2), BF16),
            jax.ShapeDtypeStruct((2, seq, 128), F32),
            jax.ShapeDtypeStruct((ts, ts), BF16),
        ],
        compiler_params=pltpu.CompilerParams(vmem_limit_bytes=VMEM_LIMIT),
        name=f"fft_tables_{seq}",
    )()


def _prep_w_in_body(w_ref, o_ref, tab_ref):
    @pl.when((pl.program_id(0) == 0) & (pl.program_id(1) == 0))
    def _():
        r = lax.broadcasted_iota(jnp.int32, (D_FFT, 2 * D_FFT), 0)
        q = lax.broadcasted_iota(jnp.int32, (D_FFT, 2 * D_FFT), 1)
        same_group = (r // HEAD_DIM) == ((q & (D_FFT - 1)) // HEAD_DIM)
        cm = ((r & (HEAD_DIM - 1)) * (q & (HEAD_DIM - 1))) & (HEAD_DIM - 1)
        ang = cm.astype(F32) * (2.0 * math.pi / HEAD_DIM)
        tab_ref[...] = jnp.where(same_group, jnp.where(q < D_FFT, jnp.cos(ang), jnp.sin(ang)), 0.0)

    o_ref[0, :, :3 * D_CONV] = w_ref[0, :, :3 * D_CONV].astype(BF16)
    o_ref[0, :, 3 * D_CONV:] = jnp.dot(
        w_ref[0, :, 3 * D_CONV:], tab_ref[...], precision=lax.Precision.HIGHEST,
        preferred_element_type=F32).astype(BF16)


def _prep_w_in(w_in):
    depth = w_in.shape[0]
    rows = 256
    return pl.pallas_call(
        _prep_w_in_body,
        grid=(depth, D_MODEL // rows),
        in_specs=[pl.BlockSpec((1, rows, w_in.shape[-1]), lambda l, i: (l, i, 0))],
        out_specs=pl.BlockSpec((1, rows, IN_EXT), lambda l, i: (l, i, 0)),
        out_shape=jax.ShapeDtypeStruct((depth, D_MODEL, IN_EXT), BF16),
        scratch_shapes=[pltpu.VMEM((D_FFT, 2 * D_FFT), F32)],
        compiler_params=_params(("arbitrary", "arbitrary")),
        name="prep_w_in",
    )(w_in)


def _mod_body(c_ref, w_ref, b_ref, o_ref):
    c = c_ref[...]
    act = c * jax.nn.sigmoid(c)
    o_ref[0] = jnp.dot(act, w_ref[0], precision=lax.Precision.HIGHEST,
                       preferred_element_type=F32) + b_ref[0]


def _modulation(c_all, w_ada, b_ada):
    depth = w_ada.shape[0]
    rows = c_all.shape[0]
    return pl.pallas_call(
        _mod_body,
        grid=(depth, N_MOD),
        in_specs=[
            pl.BlockSpec((rows, D_MODEL), lambda l, j: (0, 0)),
            pl.BlockSpec((1, D_MODEL, D_MODEL), lambda l, j: (l, 0, j)),
            pl.BlockSpec((1, 1, D_MODEL), lambda l, j: (l, 0, j)),
        ],
        out_specs=pl.BlockSpec((1, rows, D_MODEL), lambda l, j: (l, 0, j)),
        out_shape=jax.ShapeDtypeStruct((depth, rows, N_MOD * D_MODEL), F32),
        compiler_params=_params(("arbitrary", "arbitrary")),
        name="adaln_modulation",
    )(c_all, w_ada, b_ada.reshape(depth, 1, N_MOD * D_MODEL))


def _mix_in_body(x_ref, mod_ref, g_ref, w_ref, bg_ref, u_ref, a_ref):
    d = D_MODEL
    sh = mod_ref[:, 0 * d:1 * d]
    sc = mod_ref[:, 1 * d:2 * d]
    h = _rms(x_ref[0], g_ref[...] * (1.0 + sc)) + sh
    z = jnp.dot(h.astype(BF16), w_ref[...], preferred_element_type=F32)
    bg_ref[0] = z[:, :D_CONV].astype(BF16)
    u_ref[0] = (z[:, D_CONV:2 * D_CONV] * z[:, 2 * D_CONV:3 * D_CONV]).astype(BF16)
    a_ref[0, 0] = z[:, 3 * D_CONV:3 * D_CONV + D_FFT].astype(BF16)
    a_ref[0, 1] = z[:, 3 * D_CONV + D_FFT:].astype(BF16)


def _mix_in(x, mod_all, g_pre, w_in_ext, layer, row0):
    bsz, seq, d = x.shape
    tm = MIX_IN_TILE
    return pl.pallas_call(
        _mix_in_body,
        grid=(bsz, seq // tm),
        in_specs=[
            pl.BlockSpec((1, tm, d), lambda b, i: (b, i, 0)),
            pl.BlockSpec((None, None, 1, N_MOD * d), lambda b, i: (layer, row0 + b, 0, 0)),
            _resident((None, 1, d), lambda b, i: (layer, 0, 0)),
            _resident((None, d, IN_EXT), lambda b, i: (layer, 0, 0)),
        ],
        out_specs=[
            pl.BlockSpec((1, tm, D_CONV), lambda b, i: (b, i, 0)),
            pl.BlockSpec((1, tm, D_CONV), lambda b, i: (b, i, 0)),
            pl.BlockSpec((1, 2, tm, D_FFT), lambda b, i: (b, 0, i, 0)),
        ],
        out_shape=[
            jax.ShapeDtypeStruct((bsz, seq, D_CONV), BF16),
            jax.ShapeDtypeStruct((bsz, seq, D_CONV), BF16),
            jax.ShapeDtypeStruct((bsz, 2, seq, D_FFT), BF16),
        ],
        compiler_params=_params(("arbitrary", "arbitrary")),
        name="mix_in",
    )(x, mod_all, g_pre, w_in_ext)


def _seq_fft_body(a_ref, ka_ref, db_ref, tw_ref, g_ref, o_ref, y_ref, *, n1, n2):
    rows = FFT_ROWS
    half = n1 * rows
    lanes = tw_ref.shape[-1]
    for j in range(n2 // rows):
        r0 = j * rows
        x = jnp.concatenate([a_ref[0, 0, :, r0:r0 + rows, :].reshape(half, D_FFT),
                             a_ref[0, 1, :, r0:r0 + rows, :].reshape(half, D_FFT)], axis=0)
        y = jnp.dot(ka_ref[...], x, preferred_element_type=F32)
        cw = tw_ref[0, :, r0:r0 + rows, :].reshape(half, lanes)
        sw = tw_ref[1, :, r0:r0 + rows, :].reshape(half, lanes)
        yr_cols, yi_cols = [], []
        for c0 in range(0, D_FFT, lanes):
            yr, yi = y[:half, c0:c0 + lanes], y[half:, c0:c0 + lanes]
            yr_cols.append(yr * cw - yi * sw)
            yi_cols.append(yr * sw + yi * cw)
        y_ref[:, 0, r0:r0 + rows, :] = (
            jnp.concatenate(yr_cols, axis=1).astype(BF16).reshape(n1, rows, D_FFT))
        y_ref[:, 1, r0:r0 + rows, :] = (
            jnp.concatenate(yi_cols, axis=1).astype(BF16).reshape(n1, rows, D_FFT))

    for k1 in range(n1):
        f = jnp.dot(db_ref[...], y_ref[k1].reshape(2 * n2, D_FFT), preferred_element_type=F32)
        o_ref[0, k1] = _rms(f, g_ref[...]).astype(BF16)


def _seq_fft(tables, a, g_fft, layer):
    ka, db, tw, _ = tables
    bsz, _, seq, _ = a.shape
    n1, n2 = FFT_N1, seq // FFT_N1
    a5 = a.reshape(bsz, 2, n1, n2, D_FFT)
    tw4 = tw.reshape(2, n1, n2, tw.shape[-1])
    return pl.pallas_call(
        functools.partial(_seq_fft_body, n1=n1, n2=n2),
        grid=(bsz,),
        in_specs=[
            pl.BlockSpec((1, 2, n1, n2, D_FFT), lambda b: (b, 0, 0, 0, 0)),
            _resident(ka.shape, lambda b: (0, 0)),
            _resident(db.shape, lambda b: (0, 0)),
            _resident(tw4.shape, lambda b: (0, 0, 0, 0)),
            _resident((None, 1, D_FFT), lambda b: (layer, 0, 0)),
        ],
        out_specs=pl.BlockSpec((1, n1, n2, D_FFT), lambda b: (b, 0, 0, 0)),
        out_shape=jax.ShapeDtypeStruct((bsz, n1, n2, D_FFT), BF16),
        scratch_shapes=[pltpu.VMEM((n1, 2, n2, D_FFT), BF16)],
        compiler_params=_params(("arbitrary",)),
        name="seq_fft",
    )(a5, ka, db, tw4, g_fft)


def _mix_out_body(x_ref, bg_ref, u_ref, up_ref, un_ref, fn_ref, perm_ref, mod_ref, cw_ref, cb_ref,
                  gc_ref, gpm_ref, wo_ref, gpf_ref, gpo_ref, wg_ref, wu_ref, wd_ref, o_ref):
    d = D_MODEL
    tm = u_ref.shape[1]
    ts = tm // SUB_TILES
    i = pl.program_id(1)
    last = pl.num_programs(1) - 1

    mix_gain = mod_ref[:, 2 * d:3 * d] * gpm_ref[...]
    ffn_in_gain = gpf_ref[...] * (1.0 + mod_ref[:, 4 * d:5 * d])
    ffn_out_gain = mod_ref[:, 5 * d:6 * d] * gpo_ref[...]

    def rows_of(sub):
        return slice(sub * ts, (sub + 1) * ts)

    def conv_branch(sub):
        r0 = sub * ts
        u = u_ref[0, rows_of(sub), :].astype(F32)
        if sub == 0:
            prev = jnp.where(i > 0, up_ref[0, CONV_HALO - 1:CONV_HALO, :].astype(F32), 0.0)
        else:
            prev = u_ref[0, r0 - 1:r0, :].astype(F32)
        if sub == SUB_TILES - 1:
            nxt = jnp.where(i < last, un_ref[0, 0:1, :].astype(F32), 0.0)
        else:
            nxt = u_ref[0, r0 + ts:r0 + ts + 1, :].astype(F32)
        row = lax.broadcasted_iota(jnp.int32, u.shape, 0)
        u_m1 = jnp.where(row == 0, prev, pltpu.roll(u, 1, 0))
        u_p1 = jnp.where(row == ts - 1, nxt, pltpu.roll(u, ts - 1, 0))
        conv = u_m1 * cw_ref[0:1, :] + u * cw_ref[1:2, :] + u_p1 * cw_ref[2:3, :] + cb_ref[...]
        conv_n = _rms(bg_ref[0, rows_of(sub), :] * conv, gc_ref[...]).astype(BF16)
        q = ts // FFT_N1
        fn = fn_ref[0, :, sub * q:(sub + 1) * q, :].reshape(ts, D_FFT)
        fn = jnp.dot(perm_ref[...], fn, preferred_element_type=F32).astype(BF16)
        return jnp.concatenate([conv_n, fn], axis=-1)

    def out_proj(merged):
        return jnp.dot(merged, wo_ref[...], preferred_element_type=F32)

    def residual_and_ffn_input(sub, o):
        x1 = x_ref[0, rows_of(sub), :] + _rms(o, mix_gain)
        h = _rms(x1, ffn_in_gain) + mod_ref[:, 3 * d:4 * d]
        return x1, h.astype(BF16)

    def ffn_chunk(hb, ff, c0):
        c1 = min(c0 + FF_CHUNK, D_FF)
        gate = jnp.dot(hb, wg_ref[:, c0:c1], preferred_element_type=F32)
        up = jnp.dot(hb, wu_ref[:, c0:c1], preferred_element_type=F32)
        act = (gate * jax.nn.sigmoid(gate) * up).astype(BF16)
        part = jnp.dot(act, wd_ref[c0:c1, :], preferred_element_type=F32)
        return part if ff is None else ff + part

    def finish(sub, x1, ff):
        o_ref[0, rows_of(sub), :] = x1 + _rms(ff, ffn_out_gain)

    chunks = list(range(0, D_FF, FF_CHUNK))
    x1, hb = residual_and_ffn_input(0, out_proj(conv_branch(0)))
    pending = None
    for sub in range(SUB_TILES):
        more = sub + 1 < SUB_TILES
        if more:
            o_next = out_proj(conv_branch(sub + 1))
        ff = ffn_chunk(hb, None, chunks[0])
        if pending is not None:
            finish(*pending)
        if more:
            x1_next, hb_next = residual_and_ffn_input(sub + 1, o_next)
        for c0 in chunks[1:]:
            ff = ffn_chunk(hb, ff, c0)
        pending = (sub, x1, ff)
        if more:
            x1, hb = x1_next, hb_next
    finish(*pending)


def _mix_out(x, bg, u, fn, perm, mod_all, conv_w, conv_b, g_conv, g_post_mix, w_out, g_pre_ffn,
             g_post_ffn, w_gate, w_up, w_down, layer, row0):
    bsz, seq, d = x.shape
    tm = MIX_OUT_TILE
    halo_per_tile = tm // CONV_HALO
    n_halo = seq // CONV_HALO
    tok = lambda b, i: (b, i, 0)
    lay = lambda b, i: (layer, 0, 0)
    return pl.pallas_call(
        _mix_out_body,
        grid=(bsz, seq // tm),
        in_specs=[
            pl.BlockSpec((1, tm, d), tok),
            pl.BlockSpec((1, tm, D_CONV), tok),
            pl.BlockSpec((1, tm, D_CONV), tok),
            pl.BlockSpec((1, CONV_HALO, D_CONV),
                         lambda b, i: (b, jnp.maximum(i * halo_per_tile - 1, 0), 0)),
            pl.BlockSpec((1, CONV_HALO, D_CONV),
                         lambda b, i: (b, jnp.minimum((i + 1) * halo_per_tile, n_halo - 1), 0)),
            pl.BlockSpec((1, FFT_N1, tm // FFT_N1, D_FFT), lambda b, i: (b, 0, i, 0)),
            _resident(perm.shape, lambda b, i: (0, 0)),
            pl.BlockSpec((None, None, 1, N_MOD * d), lambda b, i: (layer, row0 + b, 0, 0)),
            _resident((None, 3, D_CONV), lay),
            _resident((None, 1, D_CONV), lay),
            _resident((None, 1, D_CONV), lay),
            _resident((None, 1, d), lay),
            _resident((None, d, d), lay),
            _resident((None, 1, d), lay),
            _resident((None, 1, d), lay),
            _resident((None, d, D_FF), lay),
            _resident((None, d, D_FF), lay),
            _resident((None, D_FF, d), lay),
        ],
        out_specs=pl.BlockSpec((1, tm, d), tok),
        out_shape=jax.ShapeDtypeStruct((bsz, seq, d), F32),
        compiler_params=_params(("arbitrary", "arbitrary")),
        name="mix_out",
    )(x, bg, u, u, u, fn, perm, mod_all, conv_w, conv_b, g_conv, g_post_mix, w_out, g_pre_ffn,
      g_post_ffn, w_gate, w_up, w_down)


def kernel(x_prompt, x_sample, c_prompt, c_sample, w_ada, b_ada, g_pre_mix, g_post_mix, w_in,
           conv_w, conv_b, g_conv, g_fft, w_out, g_pre_ffn, g_post_ffn, w_gate, w_up, w_down):
    depth = w_in.shape[0]
    row3 = lambda g: g.reshape(depth, 1, g.shape[-1])

    w_in_ext = _prep_w_in(w_in)
    w_out_b = w_out.astype(BF16)
    w_gate_b = w_gate.astype(BF16)
    w_up_b = w_up.astype(BF16)
    w_down_b = w_down.astype(BF16)

    c_all = jnp.concatenate([c_prompt, c_sample], axis=0)
    mod_all = _modulation(c_all, w_ada, b_ada)
    mod_all = mod_all.reshape(depth, c_all.shape[0], 1, N_MOD * D_MODEL)

    g_pre_mix3, g_post_mix3 = row3(g_pre_mix), row3(g_post_mix)
    g_conv3, g_fft3, conv_b3 = row3(g_conv), row3(g_fft), row3(conv_b)
    g_pre_ffn3, g_post_ffn3 = row3(g_pre_ffn), row3(g_post_ffn)

    outs = []
    for x, row0 in ((x_prompt, 0), (x_sample, c_prompt.shape[0])):
        tables = _fft_tables(x.shape[1])
        for layer in range(depth):
            bg, u, a = _mix_in(x, mod_all, g_pre_mix3, w_in_ext, layer, row0)
            fn = _seq_fft(tables, a, g_fft3, layer)
            x = _mix_out(x, bg, u, fn, tables[3], mod_all, conv_w, conv_b3, g_conv3, g_post_mix3, w_out_b,
                         g_pre_ffn3, g_post_ffn3, w_gate_b, w_up_b, w_down_b, layer, row0)
        outs.append(x)
    return tuple(outs)
```

```python
import functools
import math

import jax
import jax.numpy as jnp
from jax import lax
from jax.experimental import pallas as pl
from jax.experimental.pallas import tpu as pltpu

D_MODEL = 1024
D_CONV = 512
D_FFT = 512
HEAD_DIM = 64
N_FFT_GROUPS = D_FFT // HEAD_DIM
D_FF = 2816
N_MOD = 6
EPS = 1e-6
IN_EXT = 3 * D_CONV + 2 * D_FFT

F32 = jnp.float32
BF16 = jnp.bfloat16

V7X_VMEM_BYTES = 64 * 1024 * 1024
VMEM_LIMIT = V7X_VMEM_BYTES - 8 * 1024 * 1024

MIX_IN_TILE = 512
MIX_OUT_TILE = 1024
SUB_TILES = 4
BF16_SUBLANES = 16
MXU_COLS = 256
FFT_N1 = BF16_SUBLANES
FFT_ROWS = BF16_SUBLANES
FF_CHUNK = 1536
CONV_HALO = BF16_SUBLANES


def _rms(v, g):
    return v * lax.rsqrt(jnp.mean(v * v, axis=-1, keepdims=True) + EPS) * g


def _resident(block_shape, index_map):
    return pl.BlockSpec(block_shape, index_map, pipeline_mode=pl.Buffered(1))


def _params(sem):
    return pltpu.CompilerParams(dimension_semantics=sem, vmem_limit_bytes=VMEM_LIMIT)


def _fft_tables_body(ka_ref, db_ref, tw_ref, perm_ref, *, n1, n2):
    half = n1 * FFT_ROWS
    i = lax.broadcasted_iota(jnp.int32, (2 * half, 2 * half), 0)
    j = lax.broadcasted_iota(jnp.int32, (2 * half, 2 * half), 1)
    k1 = (i & (half - 1)) // FFT_ROWS
    s1 = (j & (half - 1)) // FFT_ROWS
    same_r = (i & (FFT_ROWS - 1)) == (j & (FFT_ROWS - 1))
    ang = ((k1 * s1) & (n1 - 1)).astype(F32) * (2.0 * math.pi / n1)
    pi_, pj = i >= half, j >= half
    val = jnp.where(pi_ == pj, jnp.cos(ang), jnp.where(pj, -jnp.sin(ang), jnp.sin(ang)))
    ka_ref[...] = jnp.where(same_r, val, 0.0).astype(BF16)

    k2 = lax.broadcasted_iota(jnp.int32, (n2, 2 * n2), 0)
    q = lax.broadcasted_iota(jnp.int32, (n2, 2 * n2), 1)
    ang = ((k2 * (q & (n2 - 1))) & (n2 - 1)).astype(F32) * (2.0 * math.pi / n2)
    db_ref[...] = jnp.where(q < n2, jnp.cos(ang), -jnp.sin(ang)).astype(BF16)

    t = lax.broadcasted_iota(jnp.int32, (n1 * n2, 128), 0)
    ang = ((t // n2) * (t & (n2 - 1))).astype(F32) * (2.0 * math.pi / (n1 * n2))
    tw_ref[0] = jnp.cos(ang)
    tw_ref[1] = jnp.sin(ang)

    ts = perm_ref.shape[0]
    tok = lax.broadcasted_iota(jnp.int32, (ts, ts), 0)
    src = lax.broadcasted_iota(jnp.int32, (ts, ts), 1)
    perm_ref[...] = (src == (tok & (n1 - 1)) * (ts // n1) + tok // n1).astype(BF16)


def _fft_tables(seq):
    n1, n2 = FFT_N1, seq // FFT_N1
    half = n1 * FFT_ROWS
    ts = MIX_OUT_TILE // SUB_TILES
    return pl.pallas_call(
        functools.partial(_fft_tables_body, n1=n1, n2=n2),
        out_shape=[
            jax.ShapeDtypeStruct((2 * half, 2 * half), BF16),
            jax.ShapeDtypeStruct((n2, 2 * n2), BF16),
            jax.ShapeDtypeStruct((2, seq, 128), F32),
            jax.ShapeDtypeStruct((ts, ts), BF16),
        ],
        compiler_params=pltpu.CompilerParams(vmem_limit_bytes=VMEM_LIMIT),
        name=f"fft_tables_{seq}",
    )()


def _prep_w_in_body(w_ref, o_ref, tab_ref):
    @pl.when((pl.program_id(0) == 0) & (pl.program_id(1) == 0))
    def _():
        r = lax.broadcasted_iota(jnp.int32, (D_FFT, 2 * D_FFT), 0)
        q = lax.broadcasted_iota(jnp.int32, (D_FFT, 2 * D_FFT), 1)
        same_group = (r // HEAD_DIM) == ((q & (D_FFT - 1)) // HEAD_DIM)
        cm = ((r & (HEAD_DIM - 1)) * (q & (HEAD_DIM - 1))) & (HEAD_DIM - 1)
        ang = cm.astype(F32) * (2.0 * math.pi / HEAD_DIM)
        tab_ref[...] = jnp.where(same_group, jnp.where(q < D_FFT, jnp.cos(ang), jnp.sin(ang)), 0.0)

    o_ref[0, :, :3 * D_CONV] = w_ref[0, :, :3 * D_CONV].astype(BF16)
    o_ref[0, :, 3 * D_CONV:] = jnp.dot(
        w_ref[0, :, 3 * D_CONV:], tab_ref[...], precision=lax.Precision.HIGHEST,
        preferred_element_type=F32).astype(BF16)


def _prep_w_in(w_in):
    depth = w_in.shape[0]
    rows = 256
    return pl.pallas_call(
        _prep_w_in_body,
        grid=(depth, D_MODEL // rows),
        in_specs=[pl.BlockSpec((1, rows, w_in.shape[-1]), lambda l, i: (l, i, 0))],
        out_specs=pl.BlockSpec((1, rows, IN_EXT), lambda l, i: (l, i, 0)),
        out_shape=jax.ShapeDtypeStruct((depth, D_MODEL, IN_EXT), BF16),
        scratch_shapes=[pltpu.VMEM((D_FFT, 2 * D_FFT), F32)],
        compiler_params=_params(("arbitrary", "arbitrary")),
        name="prep_w_in",
    )(w_in)


def _mod_body(c_ref, w_ref, b_ref, o_ref):
    c = c_ref[...]
    act = c * jax.nn.sigmoid(c)
    o_ref[0] = jnp.dot(act, w_ref[0], precision=lax.Precision.HIGHEST,
                       preferred_element_type=F32) + b_ref[0]


def _modulation(c_all, w_ada, b_ada):
    depth = w_ada.shape[0]
    rows = c_all.shape[0]
    return pl.pallas_call(
        _mod_body,
        grid=(depth, N_MOD),
        in_specs=[
            pl.BlockSpec((rows, D_MODEL), lambda l, j: (0, 0)),
            pl.BlockSpec((1, D_MODEL, D_MODEL), lambda l, j: (l, 0, j)),
            pl.BlockSpec((1, 1, D_MODEL), lambda l, j: (l, 0, j)),
        ],
        out_specs=pl.BlockSpec((1, rows, D_MODEL), lambda l, j: (l, 0, j)),
        out_shape=jax.ShapeDtypeStruct((depth, rows, N_MOD * D_MODEL), F32),
        compiler_params=_params(("arbitrary", "arbitrary")),
        name="adaln_modulation",
    )(c_all, w_ada, b_ada.reshape(depth, 1, N_MOD * D_MODEL))


def _mix_in_body(x_ref, mod_ref, g_ref, w_ref, bg_ref, u_ref, a_ref):
    d = D_MODEL
    sh = mod_ref[:, 0 * d:1 * d]
    sc = mod_ref[:, 1 * d:2 * d]
    h = _rms(x_ref[0], g_ref[...] * (1.0 + sc)) + sh
    z = jnp.dot(h.astype(BF16), w_ref[...], preferred_element_type=F32)
    bg_ref[0] = z[:, :D_CONV].astype(BF16)
    u_ref[0] = (z[:, D_CONV:2 * D_CONV] * z[:, 2 * D_CONV:3 * D_CONV]).astype(BF16)
    a_ref[0, 0] = z[:, 3 * D_CONV:3 * D_CONV + D_FFT].astype(BF16)
    a_ref[0, 1] = z[:, 3 * D_CONV + D_FFT:].astype(BF16)


def _mix_in(x, mod_all, g_pre, w_in_ext, layer, row0):
    bsz, seq, d = x.shape
    tm = MIX_IN_TILE
    return pl.pallas_call(
        _mix_in_body,
        grid=(bsz, seq // tm),
        in_specs=[
            pl.BlockSpec((1, tm, d), lambda b, i: (b, i, 0)),
            pl.BlockSpec((None, None, 1, N_MOD * d), lambda b, i: (layer, row0 + b, 0, 0)),
            _resident((None, 1, d), lambda b, i: (layer, 0, 0)),
            _resident((None, d, IN_EXT), lambda b, i: (layer, 0, 0)),
        ],
        out_specs=[
            pl.BlockSpec((1, tm, D_CONV), lambda b, i: (b, i, 0)),
            pl.BlockSpec((1, tm, D_CONV), lambda b, i: (b, i, 0)),
            pl.BlockSpec((1, 2, tm, D_FFT), lambda b, i: (b, 0, i, 0)),
        ],
        out_shape=[
            jax.ShapeDtypeStruct((bsz, seq, D_CONV), BF16),
            jax.ShapeDtypeStruct((bsz, seq, D_CONV), BF16),
            jax.ShapeDtypeStruct((bsz, 2, seq, D_FFT), BF16),
        ],
        compiler_params=_params(("arbitrary", "arbitrary")),
        name="mix_in",
    )(x, mod_all, g_pre, w_in_ext)


def _seq_fft_body(a_ref, ka_ref, db_ref, tw_ref, g_ref, o_ref, y_ref, *, n1, n2):
    rows = FFT_ROWS
    half = n1 * rows
    lanes = tw_ref.shape[-1]
    for j in range(n2 // rows):
        r0 = j * rows
        x = jnp.concatenate([a_ref[0, 0, :, r0:r0 + rows, :].reshape(half, D_FFT),
                             a_ref[0, 1, :, r0:r0 + rows, :].reshape(half, D_FFT)], axis=0)
        y = jnp.dot(ka_ref[...], x, preferred_element_type=F32)
        cw = tw_ref[0, :, r0:r0 + rows, :].reshape(half, lanes)
        sw = tw_ref[1, :, r0:r0 + rows, :].reshape(half, lanes)
        yr_cols, yi_cols = [], []
        for c0 in range(0, D_FFT, lanes):
            yr, yi = y[:half, c0:c0 + lanes], y[half:, c0:c0 + lanes]
            yr_cols.append(yr * cw - yi * sw)
            yi_cols.append(yr * sw + yi * cw)
        y_ref[:, 0, r0:r0 + rows, :] = (
            jnp.concatenate(yr_cols, axis=1).astype(BF16).reshape(n1, rows, D_FFT))
        y_ref[:, 1, r0:r0 + rows, :] = (
            jnp.concatenate(yi_cols, axis=1).astype(BF16).reshape(n1, rows, D_FFT))

    for k1 in range(n1):
        f = jnp.dot(db_ref[...], y_ref[k1].reshape(2 * n2, D_FFT), preferred_element_type=F32)
        o_ref[0, k1] = _rms(f, g_ref[...]).astype(BF16)


def _seq_fft(tables, a, g_fft, layer):
    ka, db, tw, _ = tables
    bsz, _, seq, _ = a.shape
    n1, n2 = FFT_N1, seq // FFT_N1
    a5 = a.reshape(bsz, 2, n1, n2, D_FFT)
    tw4 = tw.reshape(2, n1, n2, tw.shape[-1])
    return pl.pallas_call(
        functools.partial(_seq_fft_body, n1=n1, n2=n2),
        grid=(bsz,),
        in_specs=[
            pl.BlockSpec((1, 2, n1, n2, D_FFT), lambda b: (b, 0, 0, 0, 0)),
            _resident(ka.shape, lambda b: (0, 0)),
            _resident(db.shape, lambda b: (0, 0)),
            _resident(tw4.shape, lambda b: (0, 0, 0, 0)),
            _resident((None, 1, D_FFT), lambda b: (layer, 0, 0)),
        ],
        out_specs=pl.BlockSpec((1, n1, n2, D_FFT), lambda b: (b, 0, 0, 0)),
        out_shape=jax.ShapeDtypeStruct((bsz, n1, n2, D_FFT), BF16),
        scratch_shapes=[pltpu.VMEM((n1, 2, n2, D_FFT), BF16)],
        compiler_params=_params(("arbitrary",)),
        name="seq_fft",
    )(a5, ka, db, tw4, g_fft)


def _mix_out_body(x_ref, bg_ref, u_ref, up_ref, un_ref, fn_ref, perm_ref, mod_ref, cw_ref, cb_ref,
                  gc_ref, gpm_ref, wo_ref, gpf_ref, gpo_ref, wgu_ref, wd_ref, o_ref):
    d = D_MODEL
    tm = u_ref.shape[1]
    ts = tm // SUB_TILES
    i = pl.program_id(1)
    last = pl.num_programs(1) - 1

    mix_gain = mod_ref[:, 2 * d:3 * d] * gpm_ref[...]
    ffn_in_gain = gpf_ref[...] * (1.0 + mod_ref[:, 4 * d:5 * d])
    ffn_out_gain = mod_ref[:, 5 * d:6 * d] * gpo_ref[...]

    def rows_of(sub):
        return slice(sub * ts, (sub + 1) * ts)

    def conv_branch(sub):
        r0 = sub * ts
        u = u_ref[0, rows_of(sub), :].astype(F32)
        if sub == 0:
            prev = jnp.where(i > 0, up_ref[0, CONV_HALO - 1:CONV_HALO, :].astype(F32), 0.0)
        else:
            prev = u_ref[0, r0 - 1:r0, :].astype(F32)
        if sub == SUB_TILES - 1:
            nxt = jnp.where(i < last, un_ref[0, 0:1, :].astype(F32), 0.0)
        else:
            nxt = u_ref[0, r0 + ts:r0 + ts + 1, :].astype(F32)
        row = lax.broadcasted_iota(jnp.int32, u.shape, 0)
        u_m1 = jnp.where(row == 0, prev, pltpu.roll(u, 1, 0))
        u_p1 = jnp.where(row == ts - 1, nxt, pltpu.roll(u, ts - 1, 0))
        conv = u_m1 * cw_ref[0:1, :] + u * cw_ref[1:2, :] + u_p1 * cw_ref[2:3, :] + cb_ref[...]
        conv_n = _rms(bg_ref[0, rows_of(sub), :] * conv, gc_ref[...]).astype(BF16)
        q = ts // FFT_N1
        fn = fn_ref[0, :, sub * q:(sub + 1) * q, :].reshape(ts, D_FFT)
        fn = jnp.dot(perm_ref[...], fn, preferred_element_type=F32).astype(BF16)
        return jnp.concatenate([conv_n, fn], axis=-1)

    def out_proj(merged):
        return jnp.dot(merged, wo_ref[...], preferred_element_type=F32)

    def residual_and_ffn_input(sub, o):
        x1 = x_ref[0, rows_of(sub), :] + _rms(o, mix_gain)
        h = _rms(x1, ffn_in_gain) + mod_ref[:, 3 * d:4 * d]
        return x1, h.astype(BF16)

    def ffn_chunk(hb, ff, c0):
        c1 = min(c0 + FF_CHUNK, D_FF)
        gu = jnp.dot(hb, wgu_ref[:, 2 * c0:2 * c1], preferred_element_type=F32)
        acts = []
        for b0 in range(0, 2 * (c1 - c0), 2 * MXU_COLS):
            gate = gu[:, b0:b0 + MXU_COLS]
            up = gu[:, b0 + MXU_COLS:b0 + 2 * MXU_COLS]
            half = 0.5 * gate
            acts.append((half * (1.0 + jnp.tanh(half)) * up).astype(BF16))
        act = jnp.concatenate(acts, axis=-1)
        part = jnp.dot(act, wd_ref[c0:c1, :], preferred_element_type=F32)
        return part if ff is None else ff + part

    def finish(sub, x1, ff):
        o_ref[0, rows_of(sub), :] = x1 + _rms(ff, ffn_out_gain)

    chunks = list(range(0, D_FF, FF_CHUNK))
    x1, hb = residual_and_ffn_input(0, out_proj(conv_branch(0)))
    pending = None
    for sub in range(SUB_TILES):
        more = sub + 1 < SUB_TILES
        if more:
            o_next = out_proj(conv_branch(sub + 1))
        ff = ffn_chunk(hb, None, chunks[0])
        if pending is not None:
            finish(*pending)
        if more:
            x1_next, hb_next = residual_and_ffn_input(sub + 1, o_next)
        for c0 in chunks[1:]:
            ff = ffn_chunk(hb, ff, c0)
        pending = (sub, x1, ff)
        if more:
            x1, hb = x1_next, hb_next
    finish(*pending)


def _mix_out(x, bg, u, fn, perm, mod_all, conv_w, conv_b, g_conv, g_post_mix, w_out, g_pre_ffn,
             g_post_ffn, w_gate_up, w_down, layer, row0):
    bsz, seq, d = x.shape
    tm = MIX_OUT_TILE
    halo_per_tile = tm // CONV_HALO
    n_halo = seq // CONV_HALO
    tok = lambda b, i: (b, i, 0)
    lay = lambda b, i: (layer, 0, 0)
    return pl.pallas_call(
        _mix_out_body,
        grid=(bsz, seq // tm),
        in_specs=[
            pl.BlockSpec((1, tm, d), tok),
            pl.BlockSpec((1, tm, D_CONV), tok),
            pl.BlockSpec((1, tm, D_CONV), tok),
            pl.BlockSpec((1, CONV_HALO, D_CONV),
                         lambda b, i: (b, jnp.maximum(i * halo_per_tile - 1, 0), 0)),
            pl.BlockSpec((1, CONV_HALO, D_CONV),
                         lambda b, i: (b, jnp.minimum((i + 1) * halo_per_tile, n_halo - 1), 0)),
            pl.BlockSpec((1, FFT_N1, tm // FFT_N1, D_FFT), lambda b, i: (b, 0, i, 0)),
            _resident(perm.shape, lambda b, i: (0, 0)),
            pl.BlockSpec((None, None, 1, N_MOD * d), lambda b, i: (layer, row0 + b, 0, 0)),
            _resident((None, 3, D_CONV), lay),
            _resident((None, 1, D_CONV), lay),
            _resident((None, 1, D_CONV), lay),
            _resident((None, 1, d), lay),
            _resident((None, d, d), lay),
            _resident((None, 1, d), lay),
            _resident((None, 1, d), lay),
            _resident((None, d, 2 * D_FF), lay),
            _resident((None, D_FF, d), lay),
        ],
        out_specs=pl.BlockSpec((1, tm, d), tok),
        out_shape=jax.ShapeDtypeStruct((bsz, seq, d), F32),
        compiler_params=_params(("arbitrary", "arbitrary")),
        name="mix_out",
    )(x, bg, u, u, u, fn, perm, mod_all, conv_w, conv_b, g_conv, g_post_mix, w_out, g_pre_ffn,
      g_post_ffn, w_gate_up, w_down)


def kernel(x_prompt, x_sample, c_prompt, c_sample, w_ada, b_ada, g_pre_mix, g_post_mix, w_in,
           conv_w, conv_b, g_conv, g_fft, w_out, g_pre_ffn, g_post_ffn, w_gate, w_up, w_down):
    depth = w_in.shape[0]
    row3 = lambda g: g.reshape(depth, 1, g.shape[-1])

    w_in_ext = _prep_w_in(w_in)
    w_out_b = w_out.astype(BF16)
    blocks = (depth, D_MODEL, D_FF // MXU_COLS, 1, MXU_COLS)
    w_gate_up = jnp.concatenate(
        [w_gate.astype(BF16).reshape(blocks), w_up.astype(BF16).reshape(blocks)],
        axis=3).reshape(depth, D_MODEL, 2 * D_FF)
    w_down_b = w_down.astype(BF16)

    c_all = jnp.concatenate([c_prompt, c_sample], axis=0)
    mod_all = _modulation(c_all, w_ada, b_ada)
    mod_all = mod_all.reshape(depth, c_all.shape[0], 1, N_MOD * D_MODEL)

    g_pre_mix3, g_post_mix3 = row3(g_pre_mix), row3(g_post_mix)
    g_conv3, g_fft3, conv_b3 = row3(g_conv), row3(g_fft), row3(conv_b)
    g_pre_ffn3, g_post_ffn3 = row3(g_pre_ffn), row3(g_post_ffn)

    outs = []
    for x, row0 in ((x_prompt, 0), (x_sample, c_prompt.shape[0])):
        tables = _fft_tables(x.shape[1])
        for layer in range(depth):
            bg, u, a = _mix_in(x, mod_all, g_pre_mix3, w_in_ext, layer, row0)
            fn = _seq_fft(tables, a, g_fft3, layer)
            x = _mix_out(x, bg, u, fn, tables[3], mod_all, conv_w, conv_b3, g_conv3, g_post_mix3, w_out_b,
                         g_pre_ffn3, g_post_ffn3, w_gate_up, w_down_b, layer, row0)
        outs.append(x)
    return tuple(outs)
```

```python
import functools
import math

import jax
import jax.numpy as jnp
from jax import lax
from jax.experimental import pallas as pl
from jax.experimental.pallas import tpu as pltpu

D_MODEL = 1024
D_CONV = 512
D_FFT = 512
HEAD_DIM = 64
N_FFT_GROUPS = D_FFT // HEAD_DIM
D_FF = 2816
N_MOD = 6
EPS = 1e-6
IN_EXT = 3 * D_CONV + 2 * D_FFT

F32 = jnp.float32
BF16 = jnp.bfloat16

V7X_VMEM_BYTES = 64 * 1024 * 1024
VMEM_LIMIT = V7X_VMEM_BYTES - 8 * 1024 * 1024

MIX_IN_TILE = 512
MIX_OUT_TILE = 1024
SUB_TILES = 4
BF16_SUBLANES = 16
MXU_COLS = 256
FFT_N1 = BF16_SUBLANES
FFT_ROWS = BF16_SUBLANES
FF_CHUNK = 1536
CONV_HALO = BF16_SUBLANES


def _rms(v, g):
    return v * lax.rsqrt(jnp.mean(v * v, axis=-1, keepdims=True) + EPS) * g


def _resident(block_shape, index_map):
    return pl.BlockSpec(block_shape, index_map, pipeline_mode=pl.Buffered(1))


def _params(sem):
    return pltpu.CompilerParams(dimension_semantics=sem, vmem_limit_bytes=VMEM_LIMIT)


def _fft_tables_body(ka_ref, db_ref, tw_ref, perm_ref, *, n1, n2):
    half = n1 * FFT_ROWS
    i = lax.broadcasted_iota(jnp.int32, (2 * half, 2 * half), 0)
    j = lax.broadcasted_iota(jnp.int32, (2 * half, 2 * half), 1)
    k1 = (i & (half - 1)) // FFT_ROWS
    s1 = (j & (half - 1)) // FFT_ROWS
    same_r = (i & (FFT_ROWS - 1)) == (j & (FFT_ROWS - 1))
    ang = ((k1 * s1) & (n1 - 1)).astype(F32) * (2.0 * math.pi / n1)
    pi_, pj = i >= half, j >= half
    val = jnp.where(pi_ == pj, jnp.cos(ang), jnp.where(pj, -jnp.sin(ang), jnp.sin(ang)))
    ka_ref[...] = jnp.where(same_r, val, 0.0).astype(BF16)

    k2 = lax.broadcasted_iota(jnp.int32, (n2, 2 * n2), 0)
    q = lax.broadcasted_iota(jnp.int32, (n2, 2 * n2), 1)
    ang = ((k2 * (q & (n2 - 1))) & (n2 - 1)).astype(F32) * (2.0 * math.pi / n2)
    db_ref[...] = jnp.where(q < n2, jnp.cos(ang), -jnp.sin(ang)).astype(BF16)

    t = lax.broadcasted_iota(jnp.int32, (n1 * n2, 128), 0)
    ang = ((t // n2) * (t & (n2 - 1))).astype(F32) * (2.0 * math.pi / (n1 * n2))
    tw_ref[0] = jnp.cos(ang)
    tw_ref[1] = jnp.sin(ang)

    ts = perm_ref.shape[0]
    tok = lax.broadcasted_iota(jnp.int32, (ts, ts), 0)
    src = lax.broadcasted_iota(jnp.int32, (ts, ts), 1)
    perm_ref[...] = (src == (tok & (n1 - 1)) * (ts // n1) + tok // n1).astype(BF16)


def _fft_tables(seq):
    n1, n2 = FFT_N1, seq // FFT_N1
    half = n1 * FFT_ROWS
    ts = MIX_OUT_TILE // SUB_TILES
    return pl.pallas_call(
        functools.partial(_fft_tables_body, n1=n1, n2=n2),
        out_shape=[
            jax.ShapeDtypeStruct((2 * half, 2 * half), BF16),
            jax.ShapeDtypeStruct((n2, 2 * n2), BF16),
            jax.ShapeDtypeStruct((2, seq, 128), F32),
            jax.ShapeDtypeStruct((ts, ts), BF16),
        ],
        compiler_params=pltpu.CompilerParams(vmem_limit_bytes=VMEM_LIMIT),
        name=f"fft_tables_{seq}",
    )()


def _prep_w_in_body(w_ref, o_ref, tab_ref):
    @pl.when((pl.program_id(0) == 0) & (pl.program_id(1) == 0))
    def _():
        r = lax.broadcasted_iota(jnp.int32, (D_FFT, 2 * D_FFT), 0)
        q = lax.broadcasted_iota(jnp.int32, (D_FFT, 2 * D_FFT), 1)
        same_group = (r // HEAD_DIM) == ((q & (D_FFT - 1)) // HEAD_DIM)
        cm = ((r & (HEAD_DIM - 1)) * (q & (HEAD_DIM - 1))) & (HEAD_DIM - 1)
        ang = cm.astype(F32) * (2.0 * math.pi / HEAD_DIM)
        tab_ref[...] = jnp.where(same_group, jnp.where(q < D_FFT, jnp.cos(ang), jnp.sin(ang)), 0.0)

    o_ref[0, :, :3 * D_CONV] = w_ref[0, :, :3 * D_CONV].astype(BF16)
    o_ref[0, :, 3 * D_CONV:] = jnp.dot(
        w_ref[0, :, 3 * D_CONV:], tab_ref[...], precision=lax.Precision.HIGHEST,
        preferred_element_type=F32).astype(BF16)


def _prep_w_in(w_in):
    depth = w_in.shape[0]
    rows = 256
    return pl.pallas_call(
        _prep_w_in_body,
        grid=(depth, D_MODEL // rows),
        in_specs=[pl.BlockSpec((1, rows, w_in.shape[-1]), lambda l, i: (l, i, 0))],
        out_specs=pl.BlockSpec((1, rows, IN_EXT), lambda l, i: (l, i, 0)),
        out_shape=jax.ShapeDtypeStruct((depth, D_MODEL, IN_EXT), BF16),
        scratch_shapes=[pltpu.VMEM((D_FFT, 2 * D_FFT), F32)],
        compiler_params=_params(("arbitrary", "arbitrary")),
        name="prep_w_in",
    )(w_in)


def _prep_gate_up_body(g_ref, u_ref, o_ref):
    o_ref[0, :, :MXU_COLS] = g_ref[0].astype(BF16)
    o_ref[0, :, MXU_COLS:] = u_ref[0].astype(BF16)


def _prep_gate_up(w_gate, w_up):
    depth, d, ff = w_gate.shape
    src = pl.BlockSpec((1, d, MXU_COLS), lambda l, j: (l, 0, j))
    return pl.pallas_call(
        _prep_gate_up_body,
        grid=(depth, ff // MXU_COLS),
        in_specs=[src, src],
        out_specs=pl.BlockSpec((1, d, 2 * MXU_COLS), lambda l, j: (l, 0, j)),
        out_shape=jax.ShapeDtypeStruct((depth, d, 2 * ff), BF16),
        compiler_params=_params(("arbitrary", "arbitrary")),
        name="prep_gate_up",
    )(w_gate, w_up)


def _mod_body(c_ref, w_ref, b_ref, o_ref):
    c = c_ref[...]
    act = c * jax.nn.sigmoid(c)
    o_ref[0] = jnp.dot(act, w_ref[0], precision=lax.Precision.HIGHEST,
                       preferred_element_type=F32) + b_ref[0]


def _modulation(c_all, w_ada, b_ada):
    depth = w_ada.shape[0]
    rows = c_all.shape[0]
    return pl.pallas_call(
        _mod_body,
        grid=(depth, N_MOD),
        in_specs=[
            pl.BlockSpec((rows, D_MODEL), lambda l, j: (0, 0)),
            pl.BlockSpec((1, D_MODEL, D_MODEL), lambda l, j: (l, 0, j)),
            pl.BlockSpec((1, 1, D_MODEL), lambda l, j: (l, 0, j)),
        ],
        out_specs=pl.BlockSpec((1, rows, D_MODEL), lambda l, j: (l, 0, j)),
        out_shape=jax.ShapeDtypeStruct((depth, rows, N_MOD * D_MODEL), F32),
        compiler_params=_params(("arbitrary", "arbitrary")),
        name="adaln_modulation",
    )(c_all, w_ada, b_ada.reshape(depth, 1, N_MOD * D_MODEL))


def _mix_in_body(x_ref, mod_ref, g_ref, w_ref, bg_ref, u_ref, a_ref):
    d = D_MODEL
    sh = mod_ref[:, 0 * d:1 * d]
    sc = mod_ref[:, 1 * d:2 * d]
    h = _rms(x_ref[0], g_ref[...] * (1.0 + sc)) + sh
    z = jnp.dot(h.astype(BF16), w_ref[...], preferred_element_type=F32)
    bg_ref[0] = z[:, :D_CONV].astype(BF16)
    u_ref[0] = (z[:, D_CONV:2 * D_CONV] * z[:, 2 * D_CONV:3 * D_CONV]).astype(BF16)
    a_ref[0, 0] = z[:, 3 * D_CONV:3 * D_CONV + D_FFT].astype(BF16)
    a_ref[0, 1] = z[:, 3 * D_CONV + D_FFT:].astype(BF16)


def _mix_in(x, mod_all, g_pre, w_in_ext, layer, row0):
    bsz, seq, d = x.shape
    tm = MIX_IN_TILE
    return pl.pallas_call(
        _mix_in_body,
        grid=(bsz, seq // tm),
        in_specs=[
            pl.BlockSpec((1, tm, d), lambda b, i: (b, i, 0)),
            pl.BlockSpec((None, None, 1, N_MOD * d), lambda b, i: (layer, row0 + b, 0, 0)),
            _resident((None, 1, d), lambda b, i: (layer, 0, 0)),
            _resident((None, d, IN_EXT), lambda b, i: (layer, 0, 0)),
        ],
        out_specs=[
            pl.BlockSpec((1, tm, D_CONV), lambda b, i: (b, i, 0)),
            pl.BlockSpec((1, tm, D_CONV), lambda b, i: (b, i, 0)),
            pl.BlockSpec((1, 2, tm, D_FFT), lambda b, i: (b, 0, i, 0)),
        ],
        out_shape=[
            jax.ShapeDtypeStruct((bsz, seq, D_CONV), BF16),
            jax.ShapeDtypeStruct((bsz, seq, D_CONV), BF16),
            jax.ShapeDtypeStruct((bsz, 2, seq, D_FFT), BF16),
        ],
        compiler_params=_params(("arbitrary", "arbitrary")),
        name="mix_in",
    )(x, mod_all, g_pre, w_in_ext)


def _seq_fft_body(a_ref, ka_ref, db_ref, tw_ref, g_ref, o_ref, y_ref, *, n1, n2):
    rows = FFT_ROWS
    half = n1 * rows
    lanes = tw_ref.shape[-1]
    for j in range(n2 // rows):
        r0 = j * rows
        x = jnp.concatenate([a_ref[0, 0, :, r0:r0 + rows, :].reshape(half, D_FFT),
                             a_ref[0, 1, :, r0:r0 + rows, :].reshape(half, D_FFT)], axis=0)
        y = jnp.dot(ka_ref[...], x, preferred_element_type=F32)
        cw = tw_ref[0, :, r0:r0 + rows, :].reshape(half, lanes)
        sw = tw_ref[1, :, r0:r0 + rows, :].reshape(half, lanes)
        yr_cols, yi_cols = [], []
        for c0 in range(0, D_FFT, lanes):
            yr, yi = y[:half, c0:c0 + lanes], y[half:, c0:c0 + lanes]
            yr_cols.append(yr * cw - yi * sw)
            yi_cols.append(yr * sw + yi * cw)
        y_ref[:, 0, r0:r0 + rows, :] = (
            jnp.concatenate(yr_cols, axis=1).astype(BF16).reshape(n1, rows, D_FFT))
        y_ref[:, 1, r0:r0 + rows, :] = (
            jnp.concatenate(yi_cols, axis=1).astype(BF16).reshape(n1, rows, D_FFT))

    for k1 in range(n1):
        f = jnp.dot(db_ref[...], y_ref[k1].reshape(2 * n2, D_FFT), preferred_element_type=F32)
        o_ref[0, k1] = _rms(f, g_ref[...]).astype(BF16)


def _seq_fft(tables, a, g_fft, layer):
    ka, db, tw, _ = tables
    bsz, _, seq, _ = a.shape
    n1, n2 = FFT_N1, seq // FFT_N1
    a5 = a.reshape(bsz, 2, n1, n2, D_FFT)
    tw4 = tw.reshape(2, n1, n2, tw.shape[-1])
    return pl.pallas_call(
        functools.partial(_seq_fft_body, n1=n1, n2=n2),
        grid=(bsz,),
        in_specs=[
            pl.BlockSpec((1, 2, n1, n2, D_FFT), lambda b: (b, 0, 0, 0, 0)),
            _resident(ka.shape, lambda b: (0, 0)),
            _resident(db.shape, lambda b: (0, 0)),
            _resident(tw4.shape, lambda b: (0, 0, 0, 0)),
            _resident((None, 1, D_FFT), lambda b: (layer, 0, 0)),
        ],
        out_specs=pl.BlockSpec((1, n1, n2, D_FFT), lambda b: (b, 0, 0, 0)),
        out_shape=jax.ShapeDtypeStruct((bsz, n1, n2, D_FFT), BF16),
        scratch_shapes=[pltpu.VMEM((n1, 2, n2, D_FFT), BF16)],
        compiler_params=_params(("arbitrary",)),
        name="seq_fft",
    )(a5, ka, db, tw4, g_fft)


def _mix_out_body(x_ref, bg_ref, u_ref, up_ref, un_ref, fn_ref, perm_ref, mod_ref, cw_ref, cb_ref,
                  gc_ref, gpm_ref, wo_ref, gpf_ref, gpo_ref, wgu_ref, wd_ref, o_ref):
    d = D_MODEL
    tm = u_ref.shape[1]
    ts = tm // SUB_TILES
    i = pl.program_id(1)
    last = pl.num_programs(1) - 1

    mix_gain = mod_ref[:, 2 * d:3 * d] * gpm_ref[...]
    ffn_in_gain = gpf_ref[...] * (1.0 + mod_ref[:, 4 * d:5 * d])
    ffn_out_gain = mod_ref[:, 5 * d:6 * d] * gpo_ref[...]

    def rows_of(sub):
        return slice(sub * ts, (sub + 1) * ts)

    def conv_branch(sub):
        r0 = sub * ts
        u = u_ref[0, rows_of(sub), :].astype(F32)
        if sub == 0:
            prev = jnp.where(i > 0, up_ref[0, CONV_HALO - 1:CONV_HALO, :].astype(F32), 0.0)
        else:
            prev = u_ref[0, r0 - 1:r0, :].astype(F32)
        if sub == SUB_TILES - 1:
            nxt = jnp.where(i < last, un_ref[0, 0:1, :].astype(F32), 0.0)
        else:
            nxt = u_ref[0, r0 + ts:r0 + ts + 1, :].astype(F32)
        row = lax.broadcasted_iota(jnp.int32, u.shape, 0)
        u_m1 = jnp.where(row == 0, prev, pltpu.roll(u, 1, 0))
        u_p1 = jnp.where(row == ts - 1, nxt, pltpu.roll(u, ts - 1, 0))
        conv = u_m1 * cw_ref[0:1, :] + u * cw_ref[1:2, :] + u_p1 * cw_ref[2:3, :] + cb_ref[...]
        conv_n = _rms(bg_ref[0, rows_of(sub), :] * conv, gc_ref[...]).astype(BF16)
        q = ts // FFT_N1
        fn = fn_ref[0, :, sub * q:(sub + 1) * q, :].reshape(ts, D_FFT)
        fn = jnp.dot(perm_ref[...], fn, preferred_element_type=F32).astype(BF16)
        return jnp.concatenate([conv_n, fn], axis=-1)

    def out_proj(merged):
        return jnp.dot(merged, wo_ref[...], preferred_element_type=F32)

    def residual_and_ffn_input(sub, o):
        x1 = x_ref[0, rows_of(sub), :] + _rms(o, mix_gain)
        h = _rms(x1, ffn_in_gain) + mod_ref[:, 3 * d:4 * d]
        return x1, h.astype(BF16)

    def ffn_chunk(hb, ff, c0):
        c1 = min(c0 + FF_CHUNK, D_FF)
        gu = jnp.dot(hb, wgu_ref[:, 2 * c0:2 * c1], preferred_element_type=F32)
        acts = []
        for b0 in range(0, 2 * (c1 - c0), 2 * MXU_COLS):
            gate = gu[:, b0:b0 + MXU_COLS]
            up = gu[:, b0 + MXU_COLS:b0 + 2 * MXU_COLS]
            half = 0.5 * gate
            acts.append((half * (1.0 + jnp.tanh(half)) * up).astype(BF16))
        act = jnp.concatenate(acts, axis=-1)
        part = jnp.dot(act, wd_ref[c0:c1, :], preferred_element_type=F32)
        return part if ff is None else ff + part

    def finish(sub, x1, ff):
        o_ref[0, rows_of(sub), :] = x1 + _rms(ff, ffn_out_gain)

    chunks = list(range(0, D_FF, FF_CHUNK))
    x1, hb = residual_and_ffn_input(0, out_proj(conv_branch(0)))
    pending = None
    for sub in range(SUB_TILES):
        more = sub + 1 < SUB_TILES
        if more:
            o_next = out_proj(conv_branch(sub + 1))
        ff = ffn_chunk(hb, None, chunks[0])
        if pending is not None:
            finish(*pending)
        if more:
            x1_next, hb_next = residual_and_ffn_input(sub + 1, o_next)
        for c0 in chunks[1:]:
            ff = ffn_chunk(hb, ff, c0)
        pending = (sub, x1, ff)
        if more:
            x1, hb = x1_next, hb_next
    finish(*pending)


def _mix_out(x, bg, u, fn, perm, mod_all, conv_w, conv_b, g_conv, g_post_mix, w_out, g_pre_ffn,
             g_post_ffn, w_gate_up, w_down, layer, row0):
    bsz, seq, d = x.shape
    tm = MIX_OUT_TILE
    halo_per_tile = tm // CONV_HALO
    n_halo = seq // CONV_HALO
    tok = lambda b, i: (b, i, 0)
    lay = lambda b, i: (layer, 0, 0)
    return pl.pallas_call(
        _mix_out_body,
        grid=(bsz, seq // tm),
        in_specs=[
            pl.BlockSpec((1, tm, d), tok),
            pl.BlockSpec((1, tm, D_CONV), tok),
            pl.BlockSpec((1, tm, D_CONV), tok),
            pl.BlockSpec((1, CONV_HALO, D_CONV),
                         lambda b, i: (b, jnp.maximum(i * halo_per_tile - 1, 0), 0)),
            pl.BlockSpec((1, CONV_HALO, D_CONV),
                         lambda b, i: (b, jnp.minimum((i + 1) * halo_per_tile, n_halo - 1), 0)),
            pl.BlockSpec((1, FFT_N1, tm // FFT_N1, D_FFT), lambda b, i: (b, 0, i, 0)),
            _resident(perm.shape, lambda b, i: (0, 0)),
            pl.BlockSpec((None, None, 1, N_MOD * d), lambda b, i: (layer, row0 + b, 0, 0)),
            _resident((None, 3, D_CONV), lay),
            _resident((None, 1, D_CONV), lay),
            _resident((None, 1, D_CONV), lay),
            _resident((None, 1, d), lay),
            _resident((None, d, d), lay),
            _resident((None, 1, d), lay),
            _resident((None, 1, d), lay),
            _resident((None, d, 2 * D_FF), lay),
            _resident((None, D_FF, d), lay),
        ],
        out_specs=pl.BlockSpec((1, tm, d), tok),
        out_shape=jax.ShapeDtypeStruct((bsz, seq, d), F32),
        compiler_params=_params(("arbitrary", "arbitrary")),
        name="mix_out",
    )(x, bg, u, u, u, fn, perm, mod_all, conv_w, conv_b, g_conv, g_post_mix, w_out, g_pre_ffn,
      g_post_ffn, w_gate_up, w_down)


def kernel(x_prompt, x_sample, c_prompt, c_sample, w_ada, b_ada, g_pre_mix, g_post_mix, w_in,
           conv_w, conv_b, g_conv, g_fft, w_out, g_pre_ffn, g_post_ffn, w_gate, w_up, w_down):
    depth = w_in.shape[0]
    row3 = lambda g: g.reshape(depth, 1, g.shape[-1])

    w_in_ext = _prep_w_in(w_in)
    w_out_b = w_out.astype(BF16)
    w_gate_up = _prep_gate_up(w_gate, w_up)
    w_down_b = w_down.astype(BF16)

    c_all = jnp.concatenate([c_prompt, c_sample], axis=0)
    mod_all = _modulation(c_all, w_ada, b_ada)
    mod_all = mod_all.reshape(depth, c_all.shape[0], 1, N_MOD * D_MODEL)

    g_pre_mix3, g_post_mix3 = row3(g_pre_mix), row3(g_post_mix)
    g_conv3, g_fft3, conv_b3 = row3(g_conv), row3(g_fft), row3(conv_b)
    g_pre_ffn3, g_post_ffn3 = row3(g_pre_ffn), row3(g_post_ffn)

    outs = []
    for x, row0 in ((x_prompt, 0), (x_sample, c_prompt.shape[0])):
        tables = _fft_tables(x.shape[1])
        for layer in range(depth):
            bg, u, a = _mix_in(x, mod_all, g_pre_mix3, w_in_ext, layer, row0)
            fn = _seq_fft(tables, a, g_fft3, layer)
            x = _mix_out(x, bg, u, fn, tables[3], mod_all, conv_w, conv_b3, g_conv3, g_post_mix3, w_out_b,
                         g_pre_ffn3, g_post_ffn3, w_gate_up, w_down_b, layer, row0)
        outs.append(x)
    return tuple(outs)
```

```python
import functools
import math

import jax
import jax.numpy as jnp
from jax import lax
from jax.experimental import pallas as pl
from jax.experimental.pallas import tpu as pltpu

D_MODEL = 1024
D_CONV = 512
D_FFT = 512
HEAD_DIM = 64
N_FFT_GROUPS = D_FFT // HEAD_DIM
D_FF = 2816
N_MOD = 6
EPS = 1e-6
IN_EXT = 3 * D_CONV + 2 * D_FFT

F32 = jnp.float32
BF16 = jnp.bfloat16

V7X_VMEM_BYTES = 64 * 1024 * 1024
VMEM_LIMIT = V7X_VMEM_BYTES - 8 * 1024 * 1024

MIX_IN_TILE = 1024
MIX_OUT_TILE = 1024
SUB_TILES = 4
BF16_SUBLANES = 16
MXU_COLS = 256
FFT_N1 = BF16_SUBLANES
FFT_ROWS = BF16_SUBLANES
FF_CHUNK = 1536
CONV_HALO = BF16_SUBLANES


def _rms(v, g):
    return v * lax.rsqrt(jnp.mean(v * v, axis=-1, keepdims=True) + EPS) * g


def _resident(block_shape, index_map):
    return pl.BlockSpec(block_shape, index_map, pipeline_mode=pl.Buffered(1))


def _params(sem):
    return pltpu.CompilerParams(dimension_semantics=sem, vmem_limit_bytes=VMEM_LIMIT)


def _fft_tables_body(ka_ref, db_ref, tw_ref, perm_ref, *, n1, n2):
    half = n1 * FFT_ROWS
    i = lax.broadcasted_iota(jnp.int32, (2 * half, 2 * half), 0)
    j = lax.broadcasted_iota(jnp.int32, (2 * half, 2 * half), 1)
    k1 = (i & (half - 1)) // FFT_ROWS
    s1 = (j & (half - 1)) // FFT_ROWS
    same_r = (i & (FFT_ROWS - 1)) == (j & (FFT_ROWS - 1))
    ang = ((k1 * s1) & (n1 - 1)).astype(F32) * (2.0 * math.pi / n1)
    pi_, pj = i >= half, j >= half
    val = jnp.where(pi_ == pj, jnp.cos(ang), jnp.where(pj, -jnp.sin(ang), jnp.sin(ang)))
    ka_ref[...] = jnp.where(same_r, val, 0.0).astype(BF16)

    k2 = lax.broadcasted_iota(jnp.int32, (n2, 2 * n2), 0)
    q = lax.broadcasted_iota(jnp.int32, (n2, 2 * n2), 1)
    ang = ((k2 * (q & (n2 - 1))) & (n2 - 1)).astype(F32) * (2.0 * math.pi / n2)
    db_ref[...] = jnp.where(q < n2, jnp.cos(ang), -jnp.sin(ang)).astype(BF16)

    t = lax.broadcasted_iota(jnp.int32, (n1 * n2, 128), 0)
    ang = ((t // n2) * (t & (n2 - 1))).astype(F32) * (2.0 * math.pi / (n1 * n2))
    tw_ref[0] = jnp.cos(ang)
    tw_ref[1] = jnp.sin(ang)

    ts = perm_ref.shape[0]
    tok = lax.broadcasted_iota(jnp.int32, (ts, ts), 0)
    src = lax.broadcasted_iota(jnp.int32, (ts, ts), 1)
    perm_ref[...] = (src == (tok & (n1 - 1)) * (ts // n1) + tok // n1).astype(BF16)


def _fft_tables(seq):
    n1, n2 = FFT_N1, seq // FFT_N1
    half = n1 * FFT_ROWS
    ts = MIX_OUT_TILE // SUB_TILES
    return pl.pallas_call(
        functools.partial(_fft_tables_body, n1=n1, n2=n2),
        out_shape=[
            jax.ShapeDtypeStruct((2 * half, 2 * half), BF16),
            jax.ShapeDtypeStruct((n2, 2 * n2), BF16),
            jax.ShapeDtypeStruct((2, seq, 128), F32),
            jax.ShapeDtypeStruct((ts, ts), BF16),
        ],
        compiler_params=pltpu.CompilerParams(vmem_limit_bytes=VMEM_LIMIT),
        name=f"fft_tables_{seq}",
    )()


def _prep_w_in_body(w_ref, o_ref, tab_ref):
    @pl.when((pl.program_id(0) == 0) & (pl.program_id(1) == 0))
    def _():
        r = lax.broadcasted_iota(jnp.int32, (D_FFT, 2 * D_FFT), 0)
        q = lax.broadcasted_iota(jnp.int32, (D_FFT, 2 * D_FFT), 1)
        same_group = (r // HEAD_DIM) == ((q & (D_FFT - 1)) // HEAD_DIM)
        cm = ((r & (HEAD_DIM - 1)) * (q & (HEAD_DIM - 1))) & (HEAD_DIM - 1)
        ang = cm.astype(F32) * (2.0 * math.pi / HEAD_DIM)
        tab_ref[...] = jnp.where(same_group, jnp.where(q < D_FFT, jnp.cos(ang), jnp.sin(ang)), 0.0)

    o_ref[0, :, :3 * D_CONV] = w_ref[0, :, :3 * D_CONV].astype(BF16)
    o_ref[0, :, 3 * D_CONV:] = jnp.dot(
        w_ref[0, :, 3 * D_CONV:], tab_ref[...], precision=lax.Precision.HIGHEST,
        preferred_element_type=F32).astype(BF16)


def _prep_w_in(w_in):
    depth = w_in.shape[0]
    rows = 256
    return pl.pallas_call(
        _prep_w_in_body,
        grid=(depth, D_MODEL // rows),
        in_specs=[pl.BlockSpec((1, rows, w_in.shape[-1]), lambda l, i: (l, i, 0))],
        out_specs=pl.BlockSpec((1, rows, IN_EXT), lambda l, i: (l, i, 0)),
        out_shape=jax.ShapeDtypeStruct((depth, D_MODEL, IN_EXT), BF16),
        scratch_shapes=[pltpu.VMEM((D_FFT, 2 * D_FFT), F32)],
        compiler_params=_params(("arbitrary", "arbitrary")),
        name="prep_w_in",
    )(w_in)


def _prep_gate_up_body(g_ref, u_ref, o_ref):
    o_ref[0, :, :MXU_COLS] = g_ref[0].astype(BF16)
    o_ref[0, :, MXU_COLS:] = u_ref[0].astype(BF16)


def _prep_gate_up(w_gate, w_up):
    depth, d, ff = w_gate.shape
    src = pl.BlockSpec((1, d, MXU_COLS), lambda l, j: (l, 0, j))
    return pl.pallas_call(
        _prep_gate_up_body,
        grid=(depth, ff // MXU_COLS),
        in_specs=[src, src],
        out_specs=pl.BlockSpec((1, d, 2 * MXU_COLS), lambda l, j: (l, 0, j)),
        out_shape=jax.ShapeDtypeStruct((depth, d, 2 * ff), BF16),
        compiler_params=_params(("arbitrary", "arbitrary")),
        name="prep_gate_up",
    )(w_gate, w_up)


def _mod_body(c_ref, w_ref, b_ref, o_ref):
    c = c_ref[...]
    act = c * jax.nn.sigmoid(c)
    o_ref[0] = jnp.dot(act, w_ref[0], precision=lax.Precision.HIGHEST,
                       preferred_element_type=F32) + b_ref[0]


def _modulation(c_all, w_ada, b_ada):
    depth = w_ada.shape[0]
    rows = c_all.shape[0]
    return pl.pallas_call(
        _mod_body,
        grid=(depth, N_MOD),
        in_specs=[
            pl.BlockSpec((rows, D_MODEL), lambda l, j: (0, 0)),
            pl.BlockSpec((1, D_MODEL, D_MODEL), lambda l, j: (l, 0, j)),
            pl.BlockSpec((1, 1, D_MODEL), lambda l, j: (l, 0, j)),
        ],
        out_specs=pl.BlockSpec((1, rows, D_MODEL), lambda l, j: (l, 0, j)),
        out_shape=jax.ShapeDtypeStruct((depth, rows, N_MOD * D_MODEL), F32),
        compiler_params=_params(("arbitrary", "arbitrary")),
        name="adaln_modulation",
    )(c_all, w_ada, b_ada.reshape(depth, 1, N_MOD * D_MODEL))


def _mix_in_body(x_ref, mod_ref, g_ref, w_ref, bg_ref, u_ref, a_ref):
    d = D_MODEL
    sh = mod_ref[:, 0 * d:1 * d]
    sc = mod_ref[:, 1 * d:2 * d]
    h = _rms(x_ref[0], g_ref[...] * (1.0 + sc)) + sh
    z = jnp.dot(h.astype(BF16), w_ref[...], preferred_element_type=F32)
    bg_ref[0] = z[:, :D_CONV].astype(BF16)
    u_ref[0] = (z[:, D_CONV:2 * D_CONV] * z[:, 2 * D_CONV:3 * D_CONV]).astype(BF16)
    a_ref[0, 0] = z[:, 3 * D_CONV:3 * D_CONV + D_FFT].astype(BF16)
    a_ref[0, 1] = z[:, 3 * D_CONV + D_FFT:].astype(BF16)


def _mix_in(x, mod_all, g_pre, w_in_ext, layer, row0):
    bsz, seq, d = x.shape
    tm = MIX_IN_TILE
    return pl.pallas_call(
        _mix_in_body,
        grid=(bsz, seq // tm),
        in_specs=[
            pl.BlockSpec((1, tm, d), lambda b, i: (b, i, 0)),
            pl.BlockSpec((None, None, 1, N_MOD * d), lambda b, i: (layer, row0 + b, 0, 0)),
            _resident((None, 1, d), lambda b, i: (layer, 0, 0)),
            _resident((None, d, IN_EXT), lambda b, i: (layer, 0, 0)),
        ],
        out_specs=[
            pl.BlockSpec((1, tm, D_CONV), lambda b, i: (b, i, 0)),
            pl.BlockSpec((1, tm, D_CONV), lambda b, i: (b, i, 0)),
            pl.BlockSpec((1, 2, tm, D_FFT), lambda b, i: (b, 0, i, 0)),
        ],
        out_shape=[
            jax.ShapeDtypeStruct((bsz, seq, D_CONV), BF16),
            jax.ShapeDtypeStruct((bsz, seq, D_CONV), BF16),
            jax.ShapeDtypeStruct((bsz, 2, seq, D_FFT), BF16),
        ],
        compiler_params=_params(("arbitrary", "arbitrary")),
        name="mix_in",
    )(x, mod_all, g_pre, w_in_ext)


def _seq_fft_body(a_ref, ka_ref, db_ref, tw_ref, g_ref, o_ref, y_ref, *, n1, n2):
    rows = FFT_ROWS
    half = n1 * rows
    lanes = tw_ref.shape[-1]
    for j in range(n2 // rows):
        r0 = j * rows
        x = jnp.concatenate([a_ref[0, 0, :, r0:r0 + rows, :].reshape(half, D_FFT),
                             a_ref[0, 1, :, r0:r0 + rows, :].reshape(half, D_FFT)], axis=0)
        y = jnp.dot(ka_ref[...], x, preferred_element_type=F32)
        cw = tw_ref[0, :, r0:r0 + rows, :].reshape(half, lanes)
        sw = tw_ref[1, :, r0:r0 + rows, :].reshape(half, lanes)
        yr_cols, yi_cols = [], []
        for c0 in range(0, D_FFT, lanes):
            yr, yi = y[:half, c0:c0 + lanes], y[half:, c0:c0 + lanes]
            yr_cols.append(yr * cw - yi * sw)
            yi_cols.append(yr * sw + yi * cw)
        y_ref[:, 0, r0:r0 + rows, :] = (
            jnp.concatenate(yr_cols, axis=1).astype(BF16).reshape(n1, rows, D_FFT))
        y_ref[:, 1, r0:r0 + rows, :] = (
            jnp.concatenate(yi_cols, axis=1).astype(BF16).reshape(n1, rows, D_FFT))

    for k1 in range(n1):
        f = jnp.dot(db_ref[...], y_ref[k1].reshape(2 * n2, D_FFT), preferred_element_type=F32)
        o_ref[0, k1] = _rms(f, g_ref[...]).astype(BF16)


def _seq_fft(tables, a, g_fft, layer):
    ka, db, tw, _ = tables
    bsz, _, seq, _ = a.shape
    n1, n2 = FFT_N1, seq // FFT_N1
    a5 = a.reshape(bsz, 2, n1, n2, D_FFT)
    tw4 = tw.reshape(2, n1, n2, tw.shape[-1])
    return pl.pallas_call(
        functools.partial(_seq_fft_body, n1=n1, n2=n2),
        grid=(bsz,),
        in_specs=[
            pl.BlockSpec((1, 2, n1, n2, D_FFT), lambda b: (b, 0, 0, 0, 0)),
            _resident(ka.shape, lambda b: (0, 0)),
            _resident(db.shape, lambda b: (0, 0)),
            _resident(tw4.shape, lambda b: (0, 0, 0, 0)),
            _resident((None, 1, D_FFT), lambda b: (layer, 0, 0)),
        ],
        out_specs=pl.BlockSpec((1, n1, n2, D_FFT), lambda b: (b, 0, 0, 0)),
        out_shape=jax.ShapeDtypeStruct((bsz, n1, n2, D_FFT), BF16),
        scratch_shapes=[pltpu.VMEM((n1, 2, n2, D_FFT), BF16)],
        compiler_params=_params(("arbitrary",)),
        name="seq_fft",
    )(a5, ka, db, tw4, g_fft)


def _mix_out_body(x_ref, bg_ref, u_ref, up_ref, un_ref, fn_ref, perm_ref, mod_ref, cw_ref, cb_ref,
                  gc_ref, gpm_ref, wo_ref, gpf_ref, gpo_ref, wgu_ref, wd_ref, o_ref):
    d = D_MODEL
    tm = u_ref.shape[1]
    ts = tm // SUB_TILES
    i = pl.program_id(1)
    last = pl.num_programs(1) - 1

    mix_gain = mod_ref[:, 2 * d:3 * d] * gpm_ref[...]
    ffn_in_gain = gpf_ref[...] * (1.0 + mod_ref[:, 4 * d:5 * d])
    ffn_out_gain = mod_ref[:, 5 * d:6 * d] * gpo_ref[...]

    def rows_of(sub):
        return slice(sub * ts, (sub + 1) * ts)

    def conv_branch(sub):
        r0 = sub * ts
        u = u_ref[0, rows_of(sub), :].astype(F32)
        if sub == 0:
            prev = jnp.where(i > 0, up_ref[0, CONV_HALO - 1:CONV_HALO, :].astype(F32), 0.0)
        else:
            prev = u_ref[0, r0 - 1:r0, :].astype(F32)
        if sub == SUB_TILES - 1:
            nxt = jnp.where(i < last, un_ref[0, 0:1, :].astype(F32), 0.0)
        else:
            nxt = u_ref[0, r0 + ts:r0 + ts + 1, :].astype(F32)
        row = lax.broadcasted_iota(jnp.int32, u.shape, 0)
        u_m1 = jnp.where(row == 0, prev, pltpu.roll(u, 1, 0))
        u_p1 = jnp.where(row == ts - 1, nxt, pltpu.roll(u, ts - 1, 0))
        conv = u_m1 * cw_ref[0:1, :] + u * cw_ref[1:2, :] + u_p1 * cw_ref[2:3, :] + cb_ref[...]
        conv_n = _rms(bg_ref[0, rows_of(sub), :] * conv, gc_ref[...]).astype(BF16)
        q = ts // FFT_N1
        fn = fn_ref[0, :, sub * q:(sub + 1) * q, :].reshape(ts, D_FFT)
        fn = jnp.dot(perm_ref[...], fn, preferred_element_type=F32).astype(BF16)
        return jnp.concatenate([conv_n, fn], axis=-1)

    def out_proj(merged):
        return jnp.dot(merged, wo_ref[...], preferred_element_type=F32)

    def residual_and_ffn_input(sub, o):
        x1 = x_ref[0, rows_of(sub), :] + _rms(o, mix_gain)
        h = _rms(x1, ffn_in_gain) + mod_ref[:, 3 * d:4 * d]
        return x1, h.astype(BF16)

    def ffn_chunk(hb, ff, c0):
        c1 = min(c0 + FF_CHUNK, D_FF)
        gu = jnp.dot(hb, wgu_ref[:, 2 * c0:2 * c1], preferred_element_type=F32)
        acts = []
        for b0 in range(0, 2 * (c1 - c0), 2 * MXU_COLS):
            gate = gu[:, b0:b0 + MXU_COLS].astype(BF16)
            up = gu[:, b0 + MXU_COLS:b0 + 2 * MXU_COLS].astype(BF16)
            half = 0.5 * gate
            acts.append(half * (1.0 + jnp.tanh(half)) * up)
        act = jnp.concatenate(acts, axis=-1)
        part = jnp.dot(act, wd_ref[c0:c1, :], preferred_element_type=F32)
        return part if ff is None else ff + part

    def finish(sub, x1, ff):
        o_ref[0, rows_of(sub), :] = x1 + _rms(ff, ffn_out_gain)

    chunks = list(range(0, D_FF, FF_CHUNK))
    x1, hb = residual_and_ffn_input(0, out_proj(conv_branch(0)))
    pending = None
    for sub in range(SUB_TILES):
        more = sub + 1 < SUB_TILES
        if more:
            o_next = out_proj(conv_branch(sub + 1))
        ff = ffn_chunk(hb, None, chunks[0])
        if pending is not None:
            finish(*pending)
        if more:
            x1_next, hb_next = residual_and_ffn_input(sub + 1, o_next)
        for c0 in chunks[1:]:
            ff = ffn_chunk(hb, ff, c0)
        pending = (sub, x1, ff)
        if more:
            x1, hb = x1_next, hb_next
    finish(*pending)


def _mix_out(x, bg, u, fn, perm, mod_all, conv_w, conv_b, g_conv, g_post_mix, w_out, g_pre_ffn,
             g_post_ffn, w_gate_up, w_down, layer, row0):
    bsz, seq, d = x.shape
    tm = MIX_OUT_TILE
    halo_per_tile = tm // CONV_HALO
    n_halo = seq // CONV_HALO
    tok = lambda b, i: (b, i, 0)
    lay = lambda b, i: (layer, 0, 0)
    return pl.pallas_call(
        _mix_out_body,
        grid=(bsz, seq // tm),
        in_specs=[
            pl.BlockSpec((1, tm, d), tok),
            pl.BlockSpec((1, tm, D_CONV), tok),
            pl.BlockSpec((1, tm, D_CONV), tok),
            pl.BlockSpec((1, CONV_HALO, D_CONV),
                         lambda b, i: (b, jnp.maximum(i * halo_per_tile - 1, 0), 0)),
            pl.BlockSpec((1, CONV_HALO, D_CONV),
                         lambda b, i: (b, jnp.minimum((i + 1) * halo_per_tile, n_halo - 1), 0)),
            pl.BlockSpec((1, FFT_N1, tm // FFT_N1, D_FFT), lambda b, i: (b, 0, i, 0)),
            _resident(perm.shape, lambda b, i: (0, 0)),
            pl.BlockSpec((None, None, 1, N_MOD * d), lambda b, i: (layer, row0 + b, 0, 0)),
            _resident((None, 3, D_CONV), lay),
            _resident((None, 1, D_CONV), lay),
            _resident((None, 1, D_CONV), lay),
            _resident((None, 1, d), lay),
            _resident((None, d, d), lay),
            _resident((None, 1, d), lay),
            _resident((None, 1, d), lay),
            _resident((None, d, 2 * D_FF), lay),
            _resident((None, D_FF, d), lay),
        ],
        out_specs=pl.BlockSpec((1, tm, d), tok),
        out_shape=jax.ShapeDtypeStruct((bsz, seq, d), F32),
        compiler_params=_params(("arbitrary", "arbitrary")),
        name="mix_out",
    )(x, bg, u, u, u, fn, perm, mod_all, conv_w, conv_b, g_conv, g_post_mix, w_out, g_pre_ffn,
      g_post_ffn, w_gate_up, w_down)


def kernel(x_prompt, x_sample, c_prompt, c_sample, w_ada, b_ada, g_pre_mix, g_post_mix, w_in,
           conv_w, conv_b, g_conv, g_fft, w_out, g_pre_ffn, g_post_ffn, w_gate, w_up, w_down):
    depth = w_in.shape[0]
    row3 = lambda g: g.reshape(depth, 1, g.shape[-1])

    w_in_ext = _prep_w_in(w_in)
    w_out_b = w_out.astype(BF16)
    w_gate_up = _prep_gate_up(w_gate, w_up)
    w_down_b = w_down.astype(BF16)

    c_all = jnp.concatenate([c_prompt, c_sample], axis=0)
    mod_all = _modulation(c_all, w_ada, b_ada)
    mod_all = mod_all.reshape(depth, c_all.shape[0], 1, N_MOD * D_MODEL)

    g_pre_mix3, g_post_mix3 = row3(g_pre_mix), row3(g_post_mix)
    g_conv3, g_fft3, conv_b3 = row3(g_conv), row3(g_fft), row3(conv_b)
    g_pre_ffn3, g_post_ffn3 = row3(g_pre_ffn), row3(g_post_ffn)

    outs = []
    for x, row0 in ((x_prompt, 0), (x_sample, c_prompt.shape[0])):
        tables = _fft_tables(x.shape[1])
        for layer in range(depth):
            bg, u, a = _mix_in(x, mod_all, g_pre_mix3, w_in_ext, layer, row0)
            fn = _seq_fft(tables, a, g_fft3, layer)
            x = _mix_out(x, bg, u, fn, tables[3], mod_all, conv_w, conv_b3, g_conv3, g_post_mix3, w_out_b,
                         g_pre_ffn3, g_post_ffn3, w_gate_up, w_down_b, layer, row0)
        outs.append(x)
    return tuple(outs)
```

```python
import functools
import math

import jax
import jax.numpy as jnp
from jax import lax
from jax.experimental import pallas as pl
from jax.experimental.pallas import tpu as pltpu

D_MODEL = 1024
D_CONV = 512
D_FFT = 512
HEAD_DIM = 64
N_FFT_GROUPS = D_FFT // HEAD_DIM
D_FF = 2816
N_MOD = 6
EPS = 1e-6
IN_EXT = 3 * D_CONV + 2 * D_FFT

F32 = jnp.float32
BF16 = jnp.bfloat16

V7X_VMEM_BYTES = 64 * 1024 * 1024
VMEM_LIMIT = V7X_VMEM_BYTES - 8 * 1024 * 1024

MIX_IN_TILE = 2048
MIX_IN_SUB_TILES = 2
MIX_OUT_TILE = 1024
SUB_TILES = 4
BF16_SUBLANES = 16
MXU_COLS = 256
FFT_N1 = BF16_SUBLANES
FFT_ROWS = BF16_SUBLANES
FF_CHUNK = 1536
F32_SUBLANES = 8
CONV_HALO = F32_SUBLANES


def _rms(v, g):
    return v * lax.rsqrt(jnp.mean(v * v, axis=-1, keepdims=True) + EPS) * g


def _resident(block_shape, index_map):
    return pl.BlockSpec(block_shape, index_map, pipeline_mode=pl.Buffered(1))


def _params(sem):
    return pltpu.CompilerParams(dimension_semantics=sem, vmem_limit_bytes=VMEM_LIMIT)


def _fft_tables_body(ka_ref, db_ref, tw_ref, perm_ref, *, n1, n2):
    half = n1 * FFT_ROWS
    i = lax.broadcasted_iota(jnp.int32, (2 * half, 2 * half), 0)
    j = lax.broadcasted_iota(jnp.int32, (2 * half, 2 * half), 1)
    k1 = (i & (half - 1)) // FFT_ROWS
    s1 = (j & (half - 1)) // FFT_ROWS
    same_r = (i & (FFT_ROWS - 1)) == (j & (FFT_ROWS - 1))
    ang = ((k1 * s1) & (n1 - 1)).astype(F32) * (2.0 * math.pi / n1)
    pi_, pj = i >= half, j >= half
    val = jnp.where(pi_ == pj, jnp.cos(ang), jnp.where(pj, -jnp.sin(ang), jnp.sin(ang)))
    ka_ref[...] = jnp.where(same_r, val, 0.0).astype(BF16)

    k2 = lax.broadcasted_iota(jnp.int32, (n2, 2 * n2), 0)
    q = lax.broadcasted_iota(jnp.int32, (n2, 2 * n2), 1)
    ang = ((k2 * (q & (n2 - 1))) & (n2 - 1)).astype(F32) * (2.0 * math.pi / n2)
    db_ref[...] = jnp.where(q < n2, jnp.cos(ang), -jnp.sin(ang)).astype(BF16)

    t = lax.broadcasted_iota(jnp.int32, (n1 * n2, 128), 0)
    ang = ((t // n2) * (t & (n2 - 1))).astype(F32) * (2.0 * math.pi / (n1 * n2))
    tw_ref[0] = jnp.cos(ang)
    tw_ref[1] = jnp.sin(ang)

    ts = perm_ref.shape[0]
    tok = lax.broadcasted_iota(jnp.int32, (ts, ts), 0)
    src = lax.broadcasted_iota(jnp.int32, (ts, ts), 1)
    perm_ref[...] = (src == (tok & (n1 - 1)) * (ts // n1) + tok // n1).astype(BF16)


def _fft_tables(seq):
    n1, n2 = FFT_N1, seq // FFT_N1
    half = n1 * FFT_ROWS
    ts = MIX_OUT_TILE // SUB_TILES
    return pl.pallas_call(
        functools.partial(_fft_tables_body, n1=n1, n2=n2),
        out_shape=[
            jax.ShapeDtypeStruct((2 * half, 2 * half), BF16),
            jax.ShapeDtypeStruct((n2, 2 * n2), BF16),
            jax.ShapeDtypeStruct((2, seq, 128), F32),
            jax.ShapeDtypeStruct((ts, ts), BF16),
        ],
        compiler_params=pltpu.CompilerParams(vmem_limit_bytes=VMEM_LIMIT),
        name=f"fft_tables_{seq}",
    )()


def _prep_w_in_body(w_ref, o_ref, tab_ref):
    @pl.when((pl.program_id(0) == 0) & (pl.program_id(1) == 0))
    def _():
        r = lax.broadcasted_iota(jnp.int32, (D_FFT, 2 * D_FFT), 0)
        q = lax.broadcasted_iota(jnp.int32, (D_FFT, 2 * D_FFT), 1)
        same_group = (r // HEAD_DIM) == ((q & (D_FFT - 1)) // HEAD_DIM)
        cm = ((r & (HEAD_DIM - 1)) * (q & (HEAD_DIM - 1))) & (HEAD_DIM - 1)
        ang = cm.astype(F32) * (2.0 * math.pi / HEAD_DIM)
        tab_ref[...] = jnp.where(same_group, jnp.where(q < D_FFT, jnp.cos(ang), jnp.sin(ang)), 0.0)

    o_ref[0, :, :3 * D_CONV] = w_ref[0, :, :3 * D_CONV].astype(BF16)
    o_ref[0, :, 3 * D_CONV:] = jnp.dot(
        w_ref[0, :, 3 * D_CONV:], tab_ref[...], precision=lax.Precision.HIGHEST,
        preferred_element_type=F32).astype(BF16)


def _prep_w_in(w_in):
    depth = w_in.shape[0]
    rows = 256
    return pl.pallas_call(
        _prep_w_in_body,
        grid=(depth, D_MODEL // rows),
        in_specs=[pl.BlockSpec((1, rows, w_in.shape[-1]), lambda l, i: (l, i, 0))],
        out_specs=pl.BlockSpec((1, rows, IN_EXT), lambda l, i: (l, i, 0)),
        out_shape=jax.ShapeDtypeStruct((depth, D_MODEL, IN_EXT), BF16),
        scratch_shapes=[pltpu.VMEM((D_FFT, 2 * D_FFT), F32)],
        compiler_params=_params(("arbitrary", "arbitrary")),
        name="prep_w_in",
    )(w_in)


def _prep_gate_up_body(g_ref, u_ref, o_ref):
    o_ref[0, :, :MXU_COLS] = g_ref[0].astype(BF16)
    o_ref[0, :, MXU_COLS:] = u_ref[0].astype(BF16)


def _prep_gate_up(w_gate, w_up):
    depth, d, ff = w_gate.shape
    src = pl.BlockSpec((1, d, MXU_COLS), lambda l, j: (l, 0, j))
    return pl.pallas_call(
        _prep_gate_up_body,
        grid=(depth, ff // MXU_COLS),
        in_specs=[src, src],
        out_specs=pl.BlockSpec((1, d, 2 * MXU_COLS), lambda l, j: (l, 0, j)),
        out_shape=jax.ShapeDtypeStruct((depth, d, 2 * ff), BF16),
        compiler_params=_params(("arbitrary", "arbitrary")),
        name="prep_gate_up",
    )(w_gate, w_up)


def _mod_body(c_ref, w_ref, b_ref, o_ref):
    c = c_ref[...]
    act = c * jax.nn.sigmoid(c)
    o_ref[0] = jnp.dot(act, w_ref[0], precision=lax.Precision.HIGHEST,
                       preferred_element_type=F32) + b_ref[0]


def _modulation(c_all, w_ada, b_ada):
    depth = w_ada.shape[0]
    rows = c_all.shape[0]
    return pl.pallas_call(
        _mod_body,
        grid=(depth, N_MOD),
        in_specs=[
            pl.BlockSpec((rows, D_MODEL), lambda l, j: (0, 0)),
            pl.BlockSpec((1, D_MODEL, D_MODEL), lambda l, j: (l, 0, j)),
            pl.BlockSpec((1, 1, D_MODEL), lambda l, j: (l, 0, j)),
        ],
        out_specs=pl.BlockSpec((1, rows, D_MODEL), lambda l, j: (l, 0, j)),
        out_shape=jax.ShapeDtypeStruct((depth, rows, N_MOD * D_MODEL), F32),
        compiler_params=_params(("arbitrary", "arbitrary")),
        name="adaln_modulation",
    )(c_all, w_ada, b_ada.reshape(depth, 1, N_MOD * D_MODEL))


def _mix_in_body(x_ref, xp_ref, xn_ref, mod_ref, g_ref, w_ref, cw_ref, cb_ref, gc_ref,
                 cn_ref, a_ref):
    d = D_MODEL
    sh = mod_ref[:, 0 * d:1 * d]
    sc = mod_ref[:, 1 * d:2 * d]
    gain = g_ref[...] * (1.0 + sc)
    ts = x_ref.shape[1] // MIX_IN_SUB_TILES
    ext = ts + 2 * CONV_HALO
    i = pl.program_id(1)
    last = pl.num_programs(1) - 1
    for sub in range(MIX_IN_SUB_TILES):
        r0 = sub * ts
        rows = slice(r0, r0 + ts)
        before = xp_ref[0] if sub == 0 else x_ref[0, r0 - CONV_HALO:r0, :]
        after = (xn_ref[0] if sub == MIX_IN_SUB_TILES - 1
                 else x_ref[0, r0 + ts:r0 + ts + CONV_HALO, :])
        xe = jnp.concatenate([before, x_ref[0, rows, :], after], axis=0)
        h = _rms(xe, gain) + sh
        z = jnp.dot(h.astype(BF16), w_ref[...], preferred_element_type=F32)
        mid = slice(CONV_HALO, CONV_HALO + ts)

        u = z[:, D_CONV:2 * D_CONV] * z[:, 2 * D_CONV:3 * D_CONV]
        u_m1 = pltpu.roll(u, 1, 0)[mid]
        u_p1 = pltpu.roll(u, ext - 1, 0)[mid]
        row = lax.broadcasted_iota(jnp.int32, (ts, D_CONV), 0)
        if sub == 0:
            u_m1 = jnp.where((row == 0) & (i == 0), 0.0, u_m1)
        if sub == MIX_IN_SUB_TILES - 1:
            u_p1 = jnp.where((row == ts - 1) & (i == last), 0.0, u_p1)
        conv = u_m1 * cw_ref[0:1, :] + u[mid] * cw_ref[1:2, :] + u_p1 * cw_ref[2:3, :] + cb_ref[...]
        cn_ref[0, rows, :] = _rms(z[mid, :D_CONV] * conv, gc_ref[...]).astype(BF16)

        a_ref[0, 0, rows, :] = z[mid, 3 * D_CONV:3 * D_CONV + D_FFT].astype(BF16)
        a_ref[0, 1, rows, :] = z[mid, 3 * D_CONV + D_FFT:].astype(BF16)


def _mix_in(x, mod_all, g_pre, w_in_ext, conv_w, conv_b, g_conv, layer, row0):
    bsz, seq, d = x.shape
    tm = MIX_IN_TILE
    halo_per_tile = tm // CONV_HALO
    n_halo = seq // CONV_HALO
    lay = lambda b, i: (layer, 0, 0)
    return pl.pallas_call(
        _mix_in_body,
        grid=(bsz, seq // tm),
        in_specs=[
            pl.BlockSpec((1, tm, d), lambda b, i: (b, i, 0)),
            pl.BlockSpec((1, CONV_HALO, d),
                         lambda b, i: (b, jnp.maximum(i * halo_per_tile - 1, 0), 0)),
            pl.BlockSpec((1, CONV_HALO, d),
                         lambda b, i: (b, jnp.minimum((i + 1) * halo_per_tile, n_halo - 1), 0)),
            pl.BlockSpec((None, None, 1, N_MOD * d), lambda b, i: (layer, row0 + b, 0, 0)),
            _resident((None, 1, d), lay),
            _resident((None, d, IN_EXT), lay),
            _resident((None, 3, D_CONV), lay),
            _resident((None, 1, D_CONV), lay),
            _resident((None, 1, D_CONV), lay),
        ],
        out_specs=[
            pl.BlockSpec((1, tm, D_CONV), lambda b, i: (b, i, 0)),
            pl.BlockSpec((1, 2, tm, D_FFT), lambda b, i: (b, 0, i, 0)),
        ],
        out_shape=[
            jax.ShapeDtypeStruct((bsz, seq, D_CONV), BF16),
            jax.ShapeDtypeStruct((bsz, 2, seq, D_FFT), BF16),
        ],
        compiler_params=_params(("arbitrary", "arbitrary")),
        name="mix_in",
    )(x, x, x, mod_all, g_pre, w_in_ext, conv_w, conv_b, g_conv)


def _seq_fft_body(a_ref, ka_ref, db_ref, tw_ref, g_ref, o_ref, y_ref, *, n1, n2):
    rows = FFT_ROWS
    half = n1 * rows
    lanes = tw_ref.shape[-1]
    for j in range(n2 // rows):
        r0 = j * rows
        x = jnp.concatenate([a_ref[0, 0, :, r0:r0 + rows, :].reshape(half, D_FFT),
                             a_ref[0, 1, :, r0:r0 + rows, :].reshape(half, D_FFT)], axis=0)
        y = jnp.dot(ka_ref[...], x, preferred_element_type=F32)
        cw = tw_ref[0, :, r0:r0 + rows, :].reshape(half, lanes)
        sw = tw_ref[1, :, r0:r0 + rows, :].reshape(half, lanes)
        yr_cols, yi_cols = [], []
        for c0 in range(0, D_FFT, lanes):
            yr, yi = y[:half, c0:c0 + lanes], y[half:, c0:c0 + lanes]
            yr_cols.append(yr * cw - yi * sw)
            yi_cols.append(yr * sw + yi * cw)
        y_ref[:, 0, r0:r0 + rows, :] = (
            jnp.concatenate(yr_cols, axis=1).astype(BF16).reshape(n1, rows, D_FFT))
        y_ref[:, 1, r0:r0 + rows, :] = (
            jnp.concatenate(yi_cols, axis=1).astype(BF16).reshape(n1, rows, D_FFT))

    for k1 in range(n1):
        f = jnp.dot(db_ref[...], y_ref[k1].reshape(2 * n2, D_FFT), preferred_element_type=F32)
        o_ref[0, k1] = _rms(f, g_ref[...]).astype(BF16)


def _seq_fft(tables, a, g_fft, layer):
    ka, db, tw, _ = tables
    bsz, _, seq, _ = a.shape
    n1, n2 = FFT_N1, seq // FFT_N1
    a5 = a.reshape(bsz, 2, n1, n2, D_FFT)
    tw4 = tw.reshape(2, n1, n2, tw.shape[-1])
    return pl.pallas_call(
        functools.partial(_seq_fft_body, n1=n1, n2=n2),
        grid=(bsz,),
        in_specs=[
            pl.BlockSpec((1, 2, n1, n2, D_FFT), lambda b: (b, 0, 0, 0, 0)),
            _resident(ka.shape, lambda b: (0, 0)),
            _resident(db.shape, lambda b: (0, 0)),
            _resident(tw4.shape, lambda b: (0, 0, 0, 0)),
            _resident((None, 1, D_FFT), lambda b: (layer, 0, 0)),
        ],
        out_specs=pl.BlockSpec((1, n1, n2, D_FFT), lambda b: (b, 0, 0, 0)),
        out_shape=jax.ShapeDtypeStruct((bsz, n1, n2, D_FFT), BF16),
        scratch_shapes=[pltpu.VMEM((n1, 2, n2, D_FFT), BF16)],
        compiler_params=_params(("arbitrary",)),
        name="seq_fft",
    )(a5, ka, db, tw4, g_fft)


def _mix_out_body(x_ref, cn_ref, fn_ref, perm_ref, mod_ref, gpm_ref, wo_ref, gpf_ref, gpo_ref,
                  wgu_ref, wd_ref, o_ref):
    d = D_MODEL
    tm = x_ref.shape[1]
    ts = tm // SUB_TILES

    mix_gain = mod_ref[:, 2 * d:3 * d] * gpm_ref[...]
    ffn_in_gain = gpf_ref[...] * (1.0 + mod_ref[:, 4 * d:5 * d])
    ffn_out_gain = mod_ref[:, 5 * d:6 * d] * gpo_ref[...]

    def rows_of(sub):
        return slice(sub * ts, (sub + 1) * ts)

    def mixer_rows(sub):
        q = ts // FFT_N1
        fn = fn_ref[0, :, sub * q:(sub + 1) * q, :].reshape(ts, D_FFT)
        fn = jnp.dot(perm_ref[...], fn, preferred_element_type=F32).astype(BF16)
        return jnp.concatenate([cn_ref[0, rows_of(sub), :], fn], axis=-1)

    def out_proj(merged):
        return jnp.dot(merged, wo_ref[...], preferred_element_type=F32)

    def residual_and_ffn_input(sub, o):
        x1 = x_ref[0, rows_of(sub), :] + _rms(o, mix_gain)
        h = _rms(x1, ffn_in_gain) + mod_ref[:, 3 * d:4 * d]
        return x1, h.astype(BF16)

    def ffn_chunk(hb, ff, c0):
        c1 = min(c0 + FF_CHUNK, D_FF)
        gu = jnp.dot(hb, wgu_ref[:, 2 * c0:2 * c1], preferred_element_type=F32)
        acts = []
        for b0 in range(0, 2 * (c1 - c0), 2 * MXU_COLS):
            gate = gu[:, b0:b0 + MXU_COLS]
            up = gu[:, b0 + MXU_COLS:b0 + 2 * MXU_COLS]
            half = 0.5 * gate
            acts.append((half * (1.0 + jnp.tanh(half)) * up).astype(BF16))
        act = jnp.concatenate(acts, axis=-1)
        part = jnp.dot(act, wd_ref[c0:c1, :], preferred_element_type=F32)
        return part if ff is None else ff + part

    def finish(sub, x1, ff):
        o_ref[0, rows_of(sub), :] = x1 + _rms(ff, ffn_out_gain)

    chunks = list(range(0, D_FF, FF_CHUNK))
    x1, hb = residual_and_ffn_input(0, out_proj(mixer_rows(0)))
    pending = None
    for sub in range(SUB_TILES):
        more = sub + 1 < SUB_TILES
        if more:
            o_next = out_proj(mixer_rows(sub + 1))
        ff = ffn_chunk(hb, None, chunks[0])
        if pending is not None:
            finish(*pending)
        if more:
            x1_next, hb_next = residual_and_ffn_input(sub + 1, o_next)
        for c0 in chunks[1:]:
            ff = ffn_chunk(hb, ff, c0)
        pending = (sub, x1, ff)
        if more:
            x1, hb = x1_next, hb_next
    finish(*pending)


def _mix_out(x, cn, fn, perm, mod_all, g_post_mix, w_out, g_pre_ffn, g_post_ffn, w_gate_up,
             w_down, layer, row0):
    bsz, seq, d = x.shape
    tm = MIX_OUT_TILE
    tok = lambda b, i: (b, i, 0)
    lay = lambda b, i: (layer, 0, 0)
    return pl.pallas_call(
        _mix_out_body,
        grid=(bsz, seq // tm),
        in_specs=[
            pl.BlockSpec((1, tm, d), tok),
            pl.BlockSpec((1, tm, D_CONV), tok),
            pl.BlockSpec((1, FFT_N1, tm // FFT_N1, D_FFT), lambda b, i: (b, 0, i, 0)),
            _resident(perm.shape, lambda b, i: (0, 0)),
            pl.BlockSpec((None, None, 1, N_MOD * d), lambda b, i: (layer, row0 + b, 0, 0)),
            _resident((None, 1, d), lay),
            _resident((None, d, d), lay),
            _resident((None, 1, d), lay),
            _resident((None, 1, d), lay),
            _resident((None, d, 2 * D_FF), lay),
            _resident((None, D_FF, d), lay),
        ],
        out_specs=pl.BlockSpec((1, tm, d), tok),
        out_shape=jax.ShapeDtypeStruct((bsz, seq, d), F32),
        compiler_params=_params(("arbitrary", "arbitrary")),
        name="mix_out",
    )(x, cn, fn, perm, mod_all, g_post_mix, w_out, g_pre_ffn, g_post_ffn, w_gate_up, w_down)


def kernel(x_prompt, x_sample, c_prompt, c_sample, w_ada, b_ada, g_pre_mix, g_post_mix, w_in,
           conv_w, conv_b, g_conv, g_fft, w_out, g_pre_ffn, g_post_ffn, w_gate, w_up, w_down):
    depth = w_in.shape[0]
    row3 = lambda g: g.reshape(depth, 1, g.shape[-1])

    w_in_ext = _prep_w_in(w_in)
    w_out_b = w_out.astype(BF16)
    w_gate_up = _prep_gate_up(w_gate, w_up)
    w_down_b = w_down.astype(BF16)

    c_all = jnp.concatenate([c_prompt, c_sample], axis=0)
    mod_all = _modulation(c_all, w_ada, b_ada)
    mod_all = mod_all.reshape(depth, c_all.shape[0], 1, N_MOD * D_MODEL)

    g_pre_mix3, g_post_mix3 = row3(g_pre_mix), row3(g_post_mix)
    g_conv3, g_fft3, conv_b3 = row3(g_conv), row3(g_fft), row3(conv_b)
    g_pre_ffn3, g_post_ffn3 = row3(g_pre_ffn), row3(g_post_ffn)

    outs = []
    for x, row0 in ((x_prompt, 0), (x_sample, c_prompt.shape[0])):
        tables = _fft_tables(x.shape[1])
        for layer in range(depth):
            cn, a = _mix_in(x, mod_all, g_pre_mix3, w_in_ext, conv_w, conv_b3, g_conv3, layer, row0)
            fn = _seq_fft(tables, a, g_fft3, layer)
            x = _mix_out(x, cn, fn, tables[3], mod_all, g_post_mix3, w_out_b, g_pre_ffn3,
                         g_post_ffn3, w_gate_up, w_down_b, layer, row0)
        outs.append(x)
    return tuple(outs)
```

```python
import functools
import math

import jax
import jax.numpy as jnp
from jax import lax
from jax.experimental import pallas as pl
from jax.experimental.pallas import tpu as pltpu

D_MODEL = 1024
D_CONV = 512
D_FFT = 512
HEAD_DIM = 64
N_FFT_GROUPS = D_FFT // HEAD_DIM
D_FF = 2816
N_MOD = 6
EPS = 1e-6
IN_EXT = 3 * D_CONV + 2 * D_FFT

F32 = jnp.float32
BF16 = jnp.bfloat16

V7X_VMEM_BYTES = 64 * 1024 * 1024
VMEM_LIMIT = V7X_VMEM_BYTES - 8 * 1024 * 1024

MIX_IN_TILE = 2048
MIX_IN_SUB_TILES = 2
MIX_OUT_TILE = 1024
SUB_TILES = 4
BF16_SUBLANES = 16
MXU_COLS = 256
FFT_N1 = BF16_SUBLANES
FFT_ROWS = BF16_SUBLANES
FF_CHUNK = 1536
CONV_HALO = BF16_SUBLANES


def _rms(v, g):
    return v * lax.rsqrt(jnp.mean(v * v, axis=-1, keepdims=True) + EPS) * g


def _resident(block_shape, index_map):
    return pl.BlockSpec(block_shape, index_map, pipeline_mode=pl.Buffered(1))


def _params(sem):
    return pltpu.CompilerParams(dimension_semantics=sem, vmem_limit_bytes=VMEM_LIMIT)


def _fft_tables_body(ka_ref, db_ref, perm_ref, *, n1, n2):
    half = n1 * FFT_ROWS
    i = lax.broadcasted_iota(jnp.int32, (2 * half, 2 * half), 0)
    j = lax.broadcasted_iota(jnp.int32, (2 * half, 2 * half), 1)
    k1 = (i & (half - 1)) // FFT_ROWS
    s1 = (j & (half - 1)) // FFT_ROWS
    same_r = (i & (FFT_ROWS - 1)) == (j & (FFT_ROWS - 1))
    ang = ((k1 * s1) & (n1 - 1)).astype(F32) * (2.0 * math.pi / n1)
    pi_, pj = i >= half, j >= half
    val = jnp.where(pi_ == pj, jnp.cos(ang), jnp.where(pj, -jnp.sin(ang), jnp.sin(ang)))
    ka_ref[...] = jnp.where(same_r, val, 0.0).astype(BF16)

    seq = n1 * n2
    k2 = lax.broadcasted_iota(jnp.int32, (n2, 2 * n2), 0)
    q = lax.broadcasted_iota(jnp.int32, (n2, 2 * n2), 1)
    for k1 in range(n1):
        ang = (((k1 + n1 * k2) * (q & (n2 - 1))) & (seq - 1)).astype(F32) * (2.0 * math.pi / seq)
        db_ref[k1] = jnp.where(q < n2, jnp.cos(ang), -jnp.sin(ang)).astype(BF16)

    ts = perm_ref.shape[0]
    tok = lax.broadcasted_iota(jnp.int32, (ts, ts), 0)
    src = lax.broadcasted_iota(jnp.int32, (ts, ts), 1)
    perm_ref[...] = (src == (tok & (n1 - 1)) * (ts // n1) + tok // n1).astype(BF16)


def _fft_tables(seq):
    n1, n2 = FFT_N1, seq // FFT_N1
    half = n1 * FFT_ROWS
    ts = MIX_OUT_TILE // SUB_TILES
    return pl.pallas_call(
        functools.partial(_fft_tables_body, n1=n1, n2=n2),
        out_shape=[
            jax.ShapeDtypeStruct((2 * half, 2 * half), BF16),
            jax.ShapeDtypeStruct((n1, n2, 2 * n2), BF16),
            jax.ShapeDtypeStruct((ts, ts), BF16),
        ],
        compiler_params=pltpu.CompilerParams(vmem_limit_bytes=VMEM_LIMIT),
        name=f"fft_tables_{seq}",
    )()


def _prep_w_in_body(w_ref, o_ref, tab_ref):
    @pl.when((pl.program_id(0) == 0) & (pl.program_id(1) == 0))
    def _():
        r = lax.broadcasted_iota(jnp.int32, (D_FFT, 2 * D_FFT), 0)
        q = lax.broadcasted_iota(jnp.int32, (D_FFT, 2 * D_FFT), 1)
        same_group = (r // HEAD_DIM) == ((q & (D_FFT - 1)) // HEAD_DIM)
        cm = ((r & (HEAD_DIM - 1)) * (q & (HEAD_DIM - 1))) & (HEAD_DIM - 1)
        ang = cm.astype(F32) * (2.0 * math.pi / HEAD_DIM)
        tab_ref[...] = jnp.where(same_group, jnp.where(q < D_FFT, jnp.cos(ang), jnp.sin(ang)), 0.0)

    o_ref[0, :, :3 * D_CONV] = w_ref[0, :, :3 * D_CONV].astype(BF16)
    o_ref[0, :, 3 * D_CONV:] = jnp.dot(
        w_ref[0, :, 3 * D_CONV:], tab_ref[...], precision=lax.Precision.HIGHEST,
        preferred_element_type=F32).astype(BF16)


def _prep_w_in(w_in):
    depth = w_in.shape[0]
    rows = 256
    return pl.pallas_call(
        _prep_w_in_body,
        grid=(depth, D_MODEL // rows),
        in_specs=[pl.BlockSpec((1, rows, w_in.shape[-1]), lambda l, i: (l, i, 0))],
        out_specs=pl.BlockSpec((1, rows, IN_EXT), lambda l, i: (l, i, 0)),
        out_shape=jax.ShapeDtypeStruct((depth, D_MODEL, IN_EXT), BF16),
        scratch_shapes=[pltpu.VMEM((D_FFT, 2 * D_FFT), F32)],
        compiler_params=_params(("arbitrary", "arbitrary")),
        name="prep_w_in",
    )(w_in)


def _prep_gate_up_body(g_ref, u_ref, o_ref):
    o_ref[0, :, :MXU_COLS] = g_ref[0].astype(BF16)
    o_ref[0, :, MXU_COLS:] = u_ref[0].astype(BF16)


def _prep_gate_up(w_gate, w_up):
    depth, d, ff = w_gate.shape
    src = pl.BlockSpec((1, d, MXU_COLS), lambda l, j: (l, 0, j))
    return pl.pallas_call(
        _prep_gate_up_body,
        grid=(depth, ff // MXU_COLS),
        in_specs=[src, src],
        out_specs=pl.BlockSpec((1, d, 2 * MXU_COLS), lambda l, j: (l, 0, j)),
        out_shape=jax.ShapeDtypeStruct((depth, d, 2 * ff), BF16),
        compiler_params=_params(("arbitrary", "arbitrary")),
        name="prep_gate_up",
    )(w_gate, w_up)


def _mod_body(c_ref, w_ref, b_ref, o_ref):
    c = c_ref[...]
    act = c * jax.nn.sigmoid(c)
    o_ref[0] = jnp.dot(act, w_ref[0], precision=lax.Precision.HIGHEST,
                       preferred_element_type=F32) + b_ref[0]


def _modulation(c_all, w_ada, b_ada):
    depth = w_ada.shape[0]
    rows = c_all.shape[0]
    return pl.pallas_call(
        _mod_body,
        grid=(depth, N_MOD),
        in_specs=[
            pl.BlockSpec((rows, D_MODEL), lambda l, j: (0, 0)),
            pl.BlockSpec((1, D_MODEL, D_MODEL), lambda l, j: (l, 0, j)),
            pl.BlockSpec((1, 1, D_MODEL), lambda l, j: (l, 0, j)),
        ],
        out_specs=pl.BlockSpec((1, rows, D_MODEL), lambda l, j: (l, 0, j)),
        out_shape=jax.ShapeDtypeStruct((depth, rows, N_MOD * D_MODEL), F32),
        compiler_params=_params(("arbitrary", "arbitrary")),
        name="adaln_modulation",
    )(c_all, w_ada, b_ada.reshape(depth, 1, N_MOD * D_MODEL))


def _mix_in_body(x_ref, mod_ref, g_ref, w_ref, bg_ref, u_ref, a_ref):
    d = D_MODEL
    sh = mod_ref[:, 0 * d:1 * d]
    sc = mod_ref[:, 1 * d:2 * d]
    gain = g_ref[...] * (1.0 + sc)
    ts = x_ref.shape[1] // MIX_IN_SUB_TILES
    for sub in range(MIX_IN_SUB_TILES):
        rows = slice(sub * ts, (sub + 1) * ts)
        h = _rms(x_ref[0, rows, :], gain) + sh
        z = jnp.dot(h.astype(BF16), w_ref[...], preferred_element_type=F32)
        bg_ref[0, rows, :] = z[:, :D_CONV].astype(BF16)
        u_ref[0, rows, :] = (z[:, D_CONV:2 * D_CONV] * z[:, 2 * D_CONV:3 * D_CONV]).astype(BF16)
        a_ref[0, 0, rows, :] = z[:, 3 * D_CONV:3 * D_CONV + D_FFT].astype(BF16)
        a_ref[0, 1, rows, :] = z[:, 3 * D_CONV + D_FFT:].astype(BF16)


def _mix_in(x, mod_all, g_pre, w_in_ext, layer, row0):
    bsz, seq, d = x.shape
    tm = MIX_IN_TILE
    return pl.pallas_call(
        _mix_in_body,
        grid=(bsz, seq // tm),
        in_specs=[
            pl.BlockSpec((1, tm, d), lambda b, i: (b, i, 0)),
            pl.BlockSpec((None, None, 1, N_MOD * d), lambda b, i: (layer, row0 + b, 0, 0)),
            _resident((None, 1, d), lambda b, i: (layer, 0, 0)),
            _resident((None, d, IN_EXT), lambda b, i: (layer, 0, 0)),
        ],
        out_specs=[
            pl.BlockSpec((1, tm, D_CONV), lambda b, i: (b, i, 0)),
            pl.BlockSpec((1, tm, D_CONV), lambda b, i: (b, i, 0)),
            pl.BlockSpec((1, 2, tm, D_FFT), lambda b, i: (b, 0, i, 0)),
        ],
        out_shape=[
            jax.ShapeDtypeStruct((bsz, seq, D_CONV), BF16),
            jax.ShapeDtypeStruct((bsz, seq, D_CONV), BF16),
            jax.ShapeDtypeStruct((bsz, 2, seq, D_FFT), BF16),
        ],
        compiler_params=_params(("arbitrary", "arbitrary")),
        name="mix_in",
    )(x, mod_all, g_pre, w_in_ext)


def _seq_fft_body(a_ref, ka_ref, db_ref, g_ref, o_ref, y_ref, *, n1, n2):
    rows = FFT_ROWS
    half = n1 * rows
    for j in range(n2 // rows):
        r0 = j * rows
        x = jnp.concatenate([a_ref[0, 0, :, r0:r0 + rows, :].reshape(half, D_FFT),
                             a_ref[0, 1, :, r0:r0 + rows, :].reshape(half, D_FFT)], axis=0)
        y = jnp.dot(ka_ref[...], x, preferred_element_type=F32)
        y_ref[:, 0, r0:r0 + rows, :] = y[:half].astype(BF16).reshape(n1, rows, D_FFT)
        y_ref[:, 1, r0:r0 + rows, :] = y[half:].astype(BF16).reshape(n1, rows, D_FFT)

    for k1 in range(n1):
        f = jnp.dot(db_ref[k1], y_ref[k1].reshape(2 * n2, D_FFT), preferred_element_type=F32)
        o_ref[0, k1] = _rms(f, g_ref[...]).astype(BF16)


def _seq_fft(tables, a, g_fft, layer):
    ka, db, _ = tables
    bsz, _, seq, _ = a.shape
    n1, n2 = FFT_N1, seq // FFT_N1
    a5 = a.reshape(bsz, 2, n1, n2, D_FFT)
    return pl.pallas_call(
        functools.partial(_seq_fft_body, n1=n1, n2=n2),
        grid=(bsz,),
        in_specs=[
            pl.BlockSpec((1, 2, n1, n2, D_FFT), lambda b: (b, 0, 0, 0, 0)),
            _resident(ka.shape, lambda b: (0, 0)),
            _resident(db.shape, lambda b: (0, 0, 0)),
            _resident((None, 1, D_FFT), lambda b: (layer, 0, 0)),
        ],
        out_specs=pl.BlockSpec((1, n1, n2, D_FFT), lambda b: (b, 0, 0, 0)),
        out_shape=jax.ShapeDtypeStruct((bsz, n1, n2, D_FFT), BF16),
        scratch_shapes=[pltpu.VMEM((n1, 2, n2, D_FFT), BF16)],
        compiler_params=_params(("arbitrary",)),
        name="seq_fft",
    )(a5, ka, db, g_fft)


def _mix_out_body(x_ref, bg_ref, u_ref, up_ref, un_ref, fn_ref, perm_ref, mod_ref, cw_ref, cb_ref,
                  gc_ref, gpm_ref, wo_ref, gpf_ref, gpo_ref, wgu_ref, wd_ref, o_ref):
    d = D_MODEL
    tm = u_ref.shape[1]
    ts = tm // SUB_TILES
    i = pl.program_id(1)
    last = pl.num_programs(1) - 1

    mix_gain = mod_ref[:, 2 * d:3 * d] * gpm_ref[...]
    ffn_in_gain = gpf_ref[...] * (1.0 + mod_ref[:, 4 * d:5 * d])
    ffn_out_gain = mod_ref[:, 5 * d:6 * d] * gpo_ref[...]

    def rows_of(sub):
        return slice(sub * ts, (sub + 1) * ts)

    def conv_branch(sub):
        r0 = sub * ts
        u = u_ref[0, rows_of(sub), :].astype(F32)
        if sub == 0:
            prev = jnp.where(i > 0, up_ref[0, CONV_HALO - 1:CONV_HALO, :].astype(F32), 0.0)
        else:
            prev = u_ref[0, r0 - 1:r0, :].astype(F32)
        if sub == SUB_TILES - 1:
            nxt = jnp.where(i < last, un_ref[0, 0:1, :].astype(F32), 0.0)
        else:
            nxt = u_ref[0, r0 + ts:r0 + ts + 1, :].astype(F32)
        row = lax.broadcasted_iota(jnp.int32, u.shape, 0)
        u_m1 = jnp.where(row == 0, prev, pltpu.roll(u, 1, 0))
        u_p1 = jnp.where(row == ts - 1, nxt, pltpu.roll(u, ts - 1, 0))
        conv = u_m1 * cw_ref[0:1, :] + u * cw_ref[1:2, :] + u_p1 * cw_ref[2:3, :] + cb_ref[...]
        conv_n = _rms(bg_ref[0, rows_of(sub), :] * conv, gc_ref[...]).astype(BF16)
        q = ts // FFT_N1
        fn = fn_ref[0, :, sub * q:(sub + 1) * q, :].reshape(ts, D_FFT)
        fn = jnp.dot(perm_ref[...], fn, preferred_element_type=F32).astype(BF16)
        return jnp.concatenate([conv_n, fn], axis=-1)

    def out_proj(merged):
        return jnp.dot(merged, wo_ref[...], preferred_element_type=F32)

    def residual_and_ffn_input(sub, o):
        x1 = x_ref[0, rows_of(sub), :] + _rms(o, mix_gain)
        h = _rms(x1, ffn_in_gain) + mod_ref[:, 3 * d:4 * d]
        return x1, h.astype(BF16)

    def ffn_chunk(hb, ff, c0):
        c1 = min(c0 + FF_CHUNK, D_FF)
        gu = jnp.dot(hb, wgu_ref[:, 2 * c0:2 * c1], preferred_element_type=F32)
        acts = []
        for b0 in range(0, 2 * (c1 - c0), 2 * MXU_COLS):
            gate = gu[:, b0:b0 + MXU_COLS]
            up = gu[:, b0 + MXU_COLS:b0 + 2 * MXU_COLS]
            half = 0.5 * gate
            acts.append((half * (1.0 + jnp.tanh(half)) * up).astype(BF16))
        act = jnp.concatenate(acts, axis=-1)
        part = jnp.dot(act, wd_ref[c0:c1, :], preferred_element_type=F32)
        return part if ff is None else ff + part

    def finish(sub, x1, ff):
        o_ref[0, rows_of(sub), :] = x1 + _rms(ff, ffn_out_gain)

    chunks = list(range(0, D_FF, FF_CHUNK))
    x1, hb = residual_and_ffn_input(0, out_proj(conv_branch(0)))
    pending = None
    for sub in range(SUB_TILES):
        more = sub + 1 < SUB_TILES
        if more:
            o_next = out_proj(conv_branch(sub + 1))
        ff = ffn_chunk(hb, None, chunks[0])
        if pending is not None:
            finish(*pending)
        if more:
            x1_next, hb_next = residual_and_ffn_input(sub + 1, o_next)
        for c0 in chunks[1:]:
            ff = ffn_chunk(hb, ff, c0)
        pending = (sub, x1, ff)
        if more:
            x1, hb = x1_next, hb_next
    finish(*pending)


def _mix_out(x, bg, u, fn, perm, mod_all, conv_w, conv_b, g_conv, g_post_mix, w_out, g_pre_ffn,
             g_post_ffn, w_gate_up, w_down, layer, row0):
    bsz, seq, d = x.shape
    tm = MIX_OUT_TILE
    halo_per_tile = tm // CONV_HALO
    n_halo = seq // CONV_HALO
    tok = lambda b, i: (b, i, 0)
    lay = lambda b, i: (layer, 0, 0)
    return pl.pallas_call(
        _mix_out_body,
        grid=(bsz, seq // tm),
        in_specs=[
            pl.BlockSpec((1, tm, d), tok),
            pl.BlockSpec((1, tm, D_CONV), tok),
            pl.BlockSpec((1, tm, D_CONV), tok),
            pl.BlockSpec((1, CONV_HALO, D_CONV),
                         lambda b, i: (b, jnp.maximum(i * halo_per_tile - 1, 0), 0)),
            pl.BlockSpec((1, CONV_HALO, D_CONV),
                         lambda b, i: (b, jnp.minimum((i + 1) * halo_per_tile, n_halo - 1), 0)),
            pl.BlockSpec((1, FFT_N1, tm // FFT_N1, D_FFT), lambda b, i: (b, 0, i, 0)),
            _resident(perm.shape, lambda b, i: (0, 0)),
            pl.BlockSpec((None, None, 1, N_MOD * d), lambda b, i: (layer, row0 + b, 0, 0)),
            _resident((None, 3, D_CONV), lay),
            _resident((None, 1, D_CONV), lay),
            _resident((None, 1, D_CONV), lay),
            _resident((None, 1, d), lay),
            _resident((None, d, d), lay),
            _resident((None, 1, d), lay),
            _resident((None, 1, d), lay),
            _resident((None, d, 2 * D_FF), lay),
            _resident((None, D_FF, d), lay),
        ],
        out_specs=pl.BlockSpec((1, tm, d), tok),
        out_shape=jax.ShapeDtypeStruct((bsz, seq, d), F32),
        compiler_params=_params(("arbitrary", "arbitrary")),
        name="mix_out",
    )(x, bg, u, u, u, fn, perm, mod_all, conv_w, conv_b, g_conv, g_post_mix, w_out, g_pre_ffn,
      g_post_ffn, w_gate_up, w_down)


def kernel(x_prompt, x_sample, c_prompt, c_sample, w_ada, b_ada, g_pre_mix, g_post_mix, w_in,
           conv_w, conv_b, g_conv, g_fft, w_out, g_pre_ffn, g_post_ffn, w_gate, w_up, w_down):
    depth = w_in.shape[0]
    row3 = lambda g: g.reshape(depth, 1, g.shape[-1])

    w_in_ext = _prep_w_in(w_in)
    w_out_b = w_out.astype(BF16)
    w_gate_up = _prep_gate_up(w_gate, w_up)
    w_down_b = w_down.astype(BF16)

    c_all = jnp.concatenate([c_prompt, c_sample], axis=0)
    mod_all = _modulation(c_all, w_ada, b_ada)
    mod_all = mod_all.reshape(depth, c_all.shape[0], 1, N_MOD * D_MODEL)

    g_pre_mix3, g_post_mix3 = row3(g_pre_mix), row3(g_post_mix)
    g_conv3, g_fft3, conv_b3 = row3(g_conv), row3(g_fft), row3(conv_b)
    g_pre_ffn3, g_post_ffn3 = row3(g_pre_ffn), row3(g_post_ffn)

    outs = []
    for x, row0 in ((x_prompt, 0), (x_sample, c_prompt.shape[0])):
        tables = _fft_tables(x.shape[1])
        for layer in range(depth):
            bg, u, a = _mix_in(x, mod_all, g_pre_mix3, w_in_ext, layer, row0)
            fn = _seq_fft(tables, a, g_fft3, layer)
            x = _mix_out(x, bg, u, fn, tables[2], mod_all, conv_w, conv_b3, g_conv3, g_post_mix3, w_out_b,
                         g_pre_ffn3, g_post_ffn3, w_gate_up, w_down_b, layer, row0)
        outs.append(x)
    return tuple(outs)
```

```python
import functools
import math

import jax
import jax.numpy as jnp
from jax import lax
from jax.experimental import pallas as pl
from jax.experimental.pallas import tpu as pltpu

D_MODEL = 1024
D_CONV = 512
D_FFT = 512
HEAD_DIM = 64
N_FFT_GROUPS = D_FFT // HEAD_DIM
D_FF = 2816
N_MOD = 6
EPS = 1e-6
IN_EXT = 3 * D_CONV + 2 * D_FFT

F32 = jnp.float32
BF16 = jnp.bfloat16

V7X_VMEM_BYTES = 64 * 1024 * 1024
VMEM_LIMIT = V7X_VMEM_BYTES - 8 * 1024 * 1024

MIX_IN_TILE = 2048
MIX_IN_SUB_TILES = 2
MIX_OUT_TILE = 1024
SUB_TILES = 4
BF16_SUBLANES = 16
MXU_COLS = 256
FFT_N1 = BF16_SUBLANES
FFT_ROWS = BF16_SUBLANES
FF_CHUNK = 1536
CONV_HALO = BF16_SUBLANES


def _rms(v, g=None):
    y = v * lax.rsqrt(jnp.mean(v * v, axis=-1, keepdims=True) + EPS)
    return y if g is None else y * g


def _resident(block_shape, index_map):
    return pl.BlockSpec(block_shape, index_map, pipeline_mode=pl.Buffered(1))


def _params(sem):
    return pltpu.CompilerParams(dimension_semantics=sem, vmem_limit_bytes=VMEM_LIMIT)


def _fft_tables_body(ka_ref, db_ref, perm_ref, *, n1, n2):
    half = n1 * FFT_ROWS
    i = lax.broadcasted_iota(jnp.int32, (2 * half, 2 * half), 0)
    j = lax.broadcasted_iota(jnp.int32, (2 * half, 2 * half), 1)
    k1 = (i & (half - 1)) // FFT_ROWS
    s1 = (j & (half - 1)) // FFT_ROWS
    same_r = (i & (FFT_ROWS - 1)) == (j & (FFT_ROWS - 1))
    ang = ((k1 * s1) & (n1 - 1)).astype(F32) * (2.0 * math.pi / n1)
    pi_, pj = i >= half, j >= half
    val = jnp.where(pi_ == pj, jnp.cos(ang), jnp.where(pj, -jnp.sin(ang), jnp.sin(ang)))
    ka_ref[...] = jnp.where(same_r, val, 0.0).astype(BF16)

    seq = n1 * n2
    k2 = lax.broadcasted_iota(jnp.int32, (n2, n2), 0)
    s2 = lax.broadcasted_iota(jnp.int32, (n2, n2), 1)
    for k1 in range(n1):
        ang = (((k1 + n1 * k2) * s2) & (seq - 1)).astype(F32) * (2.0 * math.pi / seq)
        db_ref[k1, :, :n2] = jnp.cos(ang).astype(BF16)
        db_ref[k1, :, n2:] = (-jnp.sin(ang)).astype(BF16)

    ts = perm_ref.shape[0]
    tok = lax.broadcasted_iota(jnp.int32, (ts, ts), 0)
    src = lax.broadcasted_iota(jnp.int32, (ts, ts), 1)
    perm_ref[...] = (src == (tok & (n1 - 1)) * (ts // n1) + tok // n1).astype(BF16)


def _fft_tables(seq):
    n1, n2 = FFT_N1, seq // FFT_N1
    half = n1 * FFT_ROWS
    ts = MIX_OUT_TILE // SUB_TILES
    return pl.pallas_call(
        functools.partial(_fft_tables_body, n1=n1, n2=n2),
        out_shape=[
            jax.ShapeDtypeStruct((2 * half, 2 * half), BF16),
            jax.ShapeDtypeStruct((n1, n2, 2 * n2), BF16),
            jax.ShapeDtypeStruct((ts, ts), BF16),
        ],
        compiler_params=pltpu.CompilerParams(vmem_limit_bytes=VMEM_LIMIT),
        name=f"fft_tables_{seq}",
    )()


def _prep_w_in_body(w_ref, o_ref, tab_ref):
    @pl.when((pl.program_id(0) == 0) & (pl.program_id(1) == 0))
    def _():
        r = lax.broadcasted_iota(jnp.int32, (D_FFT, 2 * D_FFT), 0)
        q = lax.broadcasted_iota(jnp.int32, (D_FFT, 2 * D_FFT), 1)
        same_group = (r // HEAD_DIM) == ((q & (D_FFT - 1)) // HEAD_DIM)
        cm = ((r & (HEAD_DIM - 1)) * (q & (HEAD_DIM - 1))) & (HEAD_DIM - 1)
        ang = cm.astype(F32) * (2.0 * math.pi / HEAD_DIM)
        tab_ref[...] = jnp.where(same_group, jnp.where(q < D_FFT, jnp.cos(ang), jnp.sin(ang)), 0.0)

    o_ref[0, :, :3 * D_CONV] = w_ref[0, :, :3 * D_CONV].astype(BF16)
    o_ref[0, :, 3 * D_CONV:] = jnp.dot(
        w_ref[0, :, 3 * D_CONV:], tab_ref[...], precision=lax.Precision.HIGHEST,
        preferred_element_type=F32).astype(BF16)


def _prep_w_in(w_in):
    depth = w_in.shape[0]
    rows = 256
    return pl.pallas_call(
        _prep_w_in_body,
        grid=(depth, D_MODEL // rows),
        in_specs=[pl.BlockSpec((1, rows, w_in.shape[-1]), lambda l, i: (l, i, 0))],
        out_specs=pl.BlockSpec((1, rows, IN_EXT), lambda l, i: (l, i, 0)),
        out_shape=jax.ShapeDtypeStruct((depth, D_MODEL, IN_EXT), BF16),
        scratch_shapes=[pltpu.VMEM((D_FFT, 2 * D_FFT), F32)],
        compiler_params=_params(("arbitrary", "arbitrary")),
        name="prep_w_in",
    )(w_in)


def _prep_w_out_body(w_ref, g_ref, o_ref):
    o_ref[0] = (w_ref[0] * g_ref[0]).astype(BF16)


def _prep_w_out(w_out, g_rows):
    depth, d, _ = w_out.shape
    return pl.pallas_call(
        _prep_w_out_body,
        grid=(depth,),
        in_specs=[pl.BlockSpec((1, d, d), lambda l: (l, 0, 0)),
                  pl.BlockSpec((1, d, 1), lambda l: (l, 0, 0))],
        out_specs=pl.BlockSpec((1, d, d), lambda l: (l, 0, 0)),
        out_shape=jax.ShapeDtypeStruct((depth, d, d), BF16),
        compiler_params=_params(("arbitrary",)),
        name="prep_w_out",
    )(w_out, g_rows.reshape(depth, d, 1))


def _prep_gate_up_body(g_ref, u_ref, o_ref):
    o_ref[0, :, :MXU_COLS] = g_ref[0].astype(BF16)
    o_ref[0, :, MXU_COLS:] = u_ref[0].astype(BF16)


def _prep_gate_up(w_gate, w_up):
    depth, d, ff = w_gate.shape
    src = pl.BlockSpec((1, d, MXU_COLS), lambda l, j: (l, 0, j))
    return pl.pallas_call(
        _prep_gate_up_body,
        grid=(depth, ff // MXU_COLS),
        in_specs=[src, src],
        out_specs=pl.BlockSpec((1, d, 2 * MXU_COLS), lambda l, j: (l, 0, j)),
        out_shape=jax.ShapeDtypeStruct((depth, d, 2 * ff), BF16),
        compiler_params=_params(("arbitrary", "arbitrary")),
        name="prep_gate_up",
    )(w_gate, w_up)


def _mod_body(c_ref, w_ref, b_ref, o_ref):
    c = c_ref[...]
    act = c * jax.nn.sigmoid(c)
    o_ref[0] = jnp.dot(act, w_ref[0], precision=lax.Precision.HIGHEST,
                       preferred_element_type=F32) + b_ref[0]


def _modulation(c_all, w_ada, b_ada):
    depth = w_ada.shape[0]
    rows = c_all.shape[0]
    return pl.pallas_call(
        _mod_body,
        grid=(depth, N_MOD),
        in_specs=[
            pl.BlockSpec((rows, D_MODEL), lambda l, j: (0, 0)),
            pl.BlockSpec((1, D_MODEL, D_MODEL), lambda l, j: (l, 0, j)),
            pl.BlockSpec((1, 1, D_MODEL), lambda l, j: (l, 0, j)),
        ],
        out_specs=pl.BlockSpec((1, rows, D_MODEL), lambda l, j: (l, 0, j)),
        out_shape=jax.ShapeDtypeStruct((depth, rows, N_MOD * D_MODEL), F32),
        compiler_params=_params(("arbitrary", "arbitrary")),
        name="adaln_modulation",
    )(c_all, w_ada, b_ada.reshape(depth, 1, N_MOD * D_MODEL))


def _mix_in_body(x_ref, mod_ref, g_ref, w_ref, bg_ref, u_ref, a_ref):
    d = D_MODEL
    sh = mod_ref[:, 0 * d:1 * d]
    sc = mod_ref[:, 1 * d:2 * d]
    gain = g_ref[...] * (1.0 + sc)
    ts = x_ref.shape[1] // MIX_IN_SUB_TILES
    for sub in range(MIX_IN_SUB_TILES):
        rows = slice(sub * ts, (sub + 1) * ts)
        h = _rms(x_ref[0, rows, :], gain) + sh
        z = jnp.dot(h.astype(BF16), w_ref[...], preferred_element_type=F32)
        bg_ref[0, rows, :] = z[:, :D_CONV].astype(BF16)
        u_ref[0, rows, :] = (z[:, D_CONV:2 * D_CONV] * z[:, 2 * D_CONV:3 * D_CONV]).astype(BF16)
        a_ref[0, 0, rows, :] = z[:, 3 * D_CONV:3 * D_CONV + D_FFT].astype(BF16)
        a_ref[0, 1, rows, :] = z[:, 3 * D_CONV + D_FFT:].astype(BF16)


def _mix_in(x, mod_all, g_pre, w_in_ext, layer, row0):
    bsz, seq, d = x.shape
    tm = MIX_IN_TILE
    return pl.pallas_call(
        _mix_in_body,
        grid=(bsz, seq // tm),
        in_specs=[
            pl.BlockSpec((1, tm, d), lambda b, i: (b, i, 0)),
            pl.BlockSpec((None, None, 1, N_MOD * d), lambda b, i: (layer, row0 + b, 0, 0)),
            _resident((None, 1, d), lambda b, i: (layer, 0, 0)),
            _resident((None, d, IN_EXT), lambda b, i: (layer, 0, 0)),
        ],
        out_specs=[
            pl.BlockSpec((1, tm, D_CONV), lambda b, i: (b, i, 0)),
            pl.BlockSpec((1, tm, D_CONV), lambda b, i: (b, i, 0)),
            pl.BlockSpec((1, 2, tm, D_FFT), lambda b, i: (b, 0, i, 0)),
        ],
        out_shape=[
            jax.ShapeDtypeStruct((bsz, seq, D_CONV), BF16),
            jax.ShapeDtypeStruct((bsz, seq, D_CONV), BF16),
            jax.ShapeDtypeStruct((bsz, 2, seq, D_FFT), BF16),
        ],
        compiler_params=_params(("arbitrary", "arbitrary")),
        name="mix_in",
    )(x, mod_all, g_pre, w_in_ext)


def _seq_fft_body(a_ref, ka_ref, db_ref, o_ref, y_ref, *, n1, n2):
    rows = FFT_ROWS
    half = n1 * rows
    for j in range(n2 // rows):
        r0 = j * rows
        x = jnp.concatenate([a_ref[0, 0, :, r0:r0 + rows, :].reshape(half, D_FFT),
                             a_ref[0, 1, :, r0:r0 + rows, :].reshape(half, D_FFT)], axis=0)
        y = jnp.dot(ka_ref[...], x, preferred_element_type=F32)
        y_ref[:, 0, r0:r0 + rows, :] = y[:half].astype(BF16).reshape(n1, rows, D_FFT)
        y_ref[:, 1, r0:r0 + rows, :] = y[half:].astype(BF16).reshape(n1, rows, D_FFT)

    for k1 in range(n1):
        f = jnp.dot(db_ref[k1], y_ref[k1].reshape(2 * n2, D_FFT), preferred_element_type=F32)
        o_ref[0, k1] = _rms(f).astype(BF16)


def _seq_fft(tables, a):
    ka, db, _ = tables
    bsz, _, seq, _ = a.shape
    n1, n2 = FFT_N1, seq // FFT_N1
    a5 = a.reshape(bsz, 2, n1, n2, D_FFT)
    return pl.pallas_call(
        functools.partial(_seq_fft_body, n1=n1, n2=n2),
        grid=(bsz,),
        in_specs=[
            pl.BlockSpec((1, 2, n1, n2, D_FFT), lambda b: (b, 0, 0, 0, 0)),
            _resident(ka.shape, lambda b: (0, 0)),
            _resident(db.shape, lambda b: (0, 0, 0)),
        ],
        out_specs=pl.BlockSpec((1, n1, n2, D_FFT), lambda b: (b, 0, 0, 0)),
        out_shape=jax.ShapeDtypeStruct((bsz, n1, n2, D_FFT), BF16),
        scratch_shapes=[pltpu.VMEM((n1, 2, n2, D_FFT), BF16)],
        compiler_params=_params(("arbitrary",)),
        name="seq_fft",
    )(a5, ka, db)


def _mix_out_body(x_ref, bg_ref, u_ref, up_ref, un_ref, fn_ref, perm_ref, mod_ref, cw_ref, cb_ref,
                  gpm_ref, wo_ref, gpf_ref, gpo_ref, wgu_ref, wd_ref, o_ref):
    d = D_MODEL
    tm = u_ref.shape[1]
    ts = tm // SUB_TILES
    i = pl.program_id(1)
    last = pl.num_programs(1) - 1

    mix_gain = mod_ref[:, 2 * d:3 * d] * gpm_ref[...]
    ffn_in_gain = gpf_ref[...] * (1.0 + mod_ref[:, 4 * d:5 * d])
    ffn_out_gain = mod_ref[:, 5 * d:6 * d] * gpo_ref[...]

    def rows_of(sub):
        return slice(sub * ts, (sub + 1) * ts)

    def conv_branch(sub):
        r0 = sub * ts
        u = u_ref[0, rows_of(sub), :].astype(F32)
        if sub == 0:
            prev = jnp.where(i > 0, up_ref[0, CONV_HALO - 1:CONV_HALO, :].astype(F32), 0.0)
        else:
            prev = u_ref[0, r0 - 1:r0, :].astype(F32)
        if sub == SUB_TILES - 1:
            nxt = jnp.where(i < last, un_ref[0, 0:1, :].astype(F32), 0.0)
        else:
            nxt = u_ref[0, r0 + ts:r0 + ts + 1, :].astype(F32)
        row = lax.broadcasted_iota(jnp.int32, u.shape, 0)
        u_m1 = jnp.where(row == 0, prev, pltpu.roll(u, 1, 0))
        u_p1 = jnp.where(row == ts - 1, nxt, pltpu.roll(u, ts - 1, 0))
        conv = u_m1 * cw_ref[0:1, :] + u * cw_ref[1:2, :] + u_p1 * cw_ref[2:3, :] + cb_ref[...]
        conv_n = _rms(bg_ref[0, rows_of(sub), :] * conv).astype(BF16)
        q = ts // FFT_N1
        fn = fn_ref[0, :, sub * q:(sub + 1) * q, :].reshape(ts, D_FFT)
        fn = jnp.dot(perm_ref[...], fn, preferred_element_type=F32).astype(BF16)
        return jnp.concatenate([conv_n, fn], axis=-1)

    def out_proj(merged):
        return jnp.dot(merged, wo_ref[...], preferred_element_type=F32)

    def residual_and_ffn_input(sub, o):
        x1 = x_ref[0, rows_of(sub), :] + _rms(o, mix_gain)
        h = _rms(x1, ffn_in_gain) + mod_ref[:, 3 * d:4 * d]
        return x1, h.astype(BF16)

    def ffn_chunk(hb, ff, c0):
        c1 = min(c0 + FF_CHUNK, D_FF)
        gu = jnp.dot(hb, wgu_ref[:, 2 * c0:2 * c1], preferred_element_type=F32)
        acts = []
        for b0 in range(0, 2 * (c1 - c0), 2 * MXU_COLS):
            gate = gu[:, b0:b0 + MXU_COLS]
            up = gu[:, b0 + MXU_COLS:b0 + 2 * MXU_COLS]
            half = 0.5 * gate
            acts.append((half * (1.0 + jnp.tanh(half)) * up).astype(BF16))
        act = jnp.concatenate(acts, axis=-1)
        part = jnp.dot(act, wd_ref[c0:c1, :], preferred_element_type=F32)
        return part if ff is None else ff + part

    def finish(sub, x1, ff):
        o_ref[0, rows_of(sub), :] = x1 + _rms(ff, ffn_out_gain)

    chunks = list(range(0, D_FF, FF_CHUNK))
    x1, hb = residual_and_ffn_input(0, out_proj(conv_branch(0)))
    pending = None
    for sub in range(SUB_TILES):
        more = sub + 1 < SUB_TILES
        if more:
            o_next = out_proj(conv_branch(sub + 1))
        ff = ffn_chunk(hb, None, chunks[0])
        if pending is not None:
            finish(*pending)
        if more:
            x1_next, hb_next = residual_and_ffn_input(sub + 1, o_next)
        for c0 in chunks[1:]:
            ff = ffn_chunk(hb, ff, c0)
        pending = (sub, x1, ff)
        if more:
            x1, hb = x1_next, hb_next
    finish(*pending)


def _mix_out(x, bg, u, fn, perm, mod_all, conv_w, conv_b, g_post_mix, w_out, g_pre_ffn,
             g_post_ffn, w_gate_up, w_down, layer, row0):
    bsz, seq, d = x.shape
    tm = MIX_OUT_TILE
    halo_per_tile = tm // CONV_HALO
    n_halo = seq // CONV_HALO
    tok = lambda b, i: (b, i, 0)
    lay = lambda b, i: (layer, 0, 0)
    return pl.pallas_call(
        _mix_out_body,
        grid=(bsz, seq // tm),
        in_specs=[
            pl.BlockSpec((1, tm, d), tok),
            pl.BlockSpec((1, tm, D_CONV), tok),
            pl.BlockSpec((1, tm, D_CONV), tok),
            pl.BlockSpec((1, CONV_HALO, D_CONV),
                         lambda b, i: (b, jnp.maximum(i * halo_per_tile - 1, 0), 0)),
            pl.BlockSpec((1, CONV_HALO, D_CONV),
                         lambda b, i: (b, jnp.minimum((i + 1) * halo_per_tile, n_halo - 1), 0)),
            pl.BlockSpec((1, FFT_N1, tm // FFT_N1, D_FFT), lambda b, i: (b, 0, i, 0)),
            _resident(perm.shape, lambda b, i: (0, 0)),
            pl.BlockSpec((None, None, 1, N_MOD * d), lambda b, i: (layer, row0 + b, 0, 0)),
            _resident((None, 3, D_CONV), lay),
            _resident((None, 1, D_CONV), lay),
            _resident((None, 1, d), lay),
            _resident((None, d, d), lay),
            _resident((None, 1, d), lay),
            _resident((None, 1, d), lay),
            _resident((None, d, 2 * D_FF), lay),
            _resident((None, D_FF, d), lay),
        ],
        out_specs=pl.BlockSpec((1, tm, d), tok),
        out_shape=jax.ShapeDtypeStruct((bsz, seq, d), F32),
        compiler_params=_params(("arbitrary", "arbitrary")),
        name="mix_out",
    )(x, bg, u, u, u, fn, perm, mod_all, conv_w, conv_b, g_post_mix, w_out, g_pre_ffn,
      g_post_ffn, w_gate_up, w_down)


def kernel(x_prompt, x_sample, c_prompt, c_sample, w_ada, b_ada, g_pre_mix, g_post_mix, w_in,
           conv_w, conv_b, g_conv, g_fft, w_out, g_pre_ffn, g_post_ffn, w_gate, w_up, w_down):
    depth = w_in.shape[0]
    row3 = lambda g: g.reshape(depth, 1, g.shape[-1])

    w_in_ext = _prep_w_in(w_in)
    w_out_b = _prep_w_out(w_out, jnp.concatenate([g_conv, g_fft], axis=-1))
    w_gate_up = _prep_gate_up(w_gate, w_up)
    w_down_b = w_down.astype(BF16)

    c_all = jnp.concatenate([c_prompt, c_sample], axis=0)
    mod_all = _modulation(c_all, w_ada, b_ada)
    mod_all = mod_all.reshape(depth, c_all.shape[0], 1, N_MOD * D_MODEL)

    g_pre_mix3, g_post_mix3 = row3(g_pre_mix), row3(g_post_mix)
    conv_b3 = row3(conv_b)
    g_pre_ffn3, g_post_ffn3 = row3(g_pre_ffn), row3(g_post_ffn)

    outs = []
    for x, row0 in ((x_prompt, 0), (x_sample, c_prompt.shape[0])):
        tables = _fft_tables(x.shape[1])
        for layer in range(depth):
            bg, u, a = _mix_in(x, mod_all, g_pre_mix3, w_in_ext, layer, row0)
            fn = _seq_fft(tables, a)
            x = _mix_out(x, bg, u, fn, tables[2], mod_all, conv_w, conv_b3, g_post_mix3, w_out_b,
                         g_pre_ffn3, g_post_ffn3, w_gate_up, w_down_b, layer, row0)
        outs.append(x)
    return tuple(outs)
```

```python
import functools
import math

import jax
import jax.numpy as jnp
from jax import lax
from jax.experimental import pallas as pl
from jax.experimental.pallas import tpu as pltpu

D_MODEL = 1024
D_CONV = 512
D_FFT = 512
HEAD_DIM = 64
N_FFT_GROUPS = D_FFT // HEAD_DIM
D_FF = 2816
N_MOD = 6
EPS = 1e-6
IN_EXT = 3 * D_CONV + 2 * D_FFT

F32 = jnp.float32
BF16 = jnp.bfloat16

V7X_VMEM_BYTES = 64 * 1024 * 1024
VMEM_LIMIT = V7X_VMEM_BYTES - 8 * 1024 * 1024

MIX_IN_TILE = 2048
MIX_IN_SUB_TILES = 2
MIX_OUT_TILE = 1024
SUB_TILES = 4
BF16_SUBLANES = 16
MXU_COLS = 256
FFT_N1 = BF16_SUBLANES
FFT_ROWS = BF16_SUBLANES
FF_CHUNK = 1536
CONV_HALO = BF16_SUBLANES


def _rms(v, g=None):
    y = v * lax.rsqrt(jnp.mean(v * v, axis=-1, keepdims=True) + EPS)
    return y if g is None else y * g


def _resident(block_shape, index_map):
    return pl.BlockSpec(block_shape, index_map, pipeline_mode=pl.Buffered(1))


def _params(sem):
    return pltpu.CompilerParams(dimension_semantics=sem, vmem_limit_bytes=VMEM_LIMIT)


def _fft_tables_body(ka_ref, db_ref, perm_ref, *, n1, n2):
    half = n1 * FFT_ROWS
    i = lax.broadcasted_iota(jnp.int32, (2 * half, 2 * half), 0)
    j = lax.broadcasted_iota(jnp.int32, (2 * half, 2 * half), 1)
    k1 = (i & (half - 1)) // FFT_ROWS
    s1 = (j & (half - 1)) // FFT_ROWS
    same_r = (i & (FFT_ROWS - 1)) == (j & (FFT_ROWS - 1))
    ang = ((k1 * s1) & (n1 - 1)).astype(F32) * (2.0 * math.pi / n1)
    pi_, pj = i >= half, j >= half
    val = jnp.where(pi_ == pj, jnp.cos(ang), jnp.where(pj, -jnp.sin(ang), jnp.sin(ang)))
    ka_ref[...] = jnp.where(same_r, val, 0.0).astype(BF16)

    seq = n1 * n2
    k2 = lax.broadcasted_iota(jnp.int32, (n2, n2), 0)
    s2 = lax.broadcasted_iota(jnp.int32, (n2, n2), 1)
    for k1 in range(n1):
        ang = (((k1 + n1 * k2) * s2) & (seq - 1)).astype(F32) * (2.0 * math.pi / seq)
        db_ref[k1, :, :n2] = jnp.cos(ang).astype(BF16)
        db_ref[k1, :, n2:] = (-jnp.sin(ang)).astype(BF16)

    ts = perm_ref.shape[0]
    tok = lax.broadcasted_iota(jnp.int32, (ts, ts), 0)
    src = lax.broadcasted_iota(jnp.int32, (ts, ts), 1)
    perm_ref[...] = (src == (tok & (n1 - 1)) * (ts // n1) + tok // n1).astype(BF16)


def _fft_tables(seq):
    n1, n2 = FFT_N1, seq // FFT_N1
    half = n1 * FFT_ROWS
    ts = MIX_OUT_TILE // SUB_TILES
    return pl.pallas_call(
        functools.partial(_fft_tables_body, n1=n1, n2=n2),
        out_shape=[
            jax.ShapeDtypeStruct((2 * half, 2 * half), BF16),
            jax.ShapeDtypeStruct((n1, n2, 2 * n2), BF16),
            jax.ShapeDtypeStruct((ts, ts), BF16),
        ],
        compiler_params=pltpu.CompilerParams(vmem_limit_bytes=VMEM_LIMIT),
        name=f"fft_tables_{seq}",
    )()


def _prep_w_in_body(w_ref, o_ref, tab_ref):
    @pl.when((pl.program_id(0) == 0) & (pl.program_id(1) == 0))
    def _():
        r = lax.broadcasted_iota(jnp.int32, (D_FFT, 2 * D_FFT), 0)
        q = lax.broadcasted_iota(jnp.int32, (D_FFT, 2 * D_FFT), 1)
        same_group = (r // HEAD_DIM) == ((q & (D_FFT - 1)) // HEAD_DIM)
        cm = ((r & (HEAD_DIM - 1)) * (q & (HEAD_DIM - 1))) & (HEAD_DIM - 1)
        ang = cm.astype(F32) * (2.0 * math.pi / HEAD_DIM)
        tab_ref[...] = jnp.where(same_group, jnp.where(q < D_FFT, jnp.cos(ang), jnp.sin(ang)), 0.0)

    o_ref[0, :, :3 * D_CONV] = w_ref[0, :, :3 * D_CONV].astype(BF16)
    o_ref[0, :, 3 * D_CONV:] = jnp.dot(
        w_ref[0, :, 3 * D_CONV:], tab_ref[...], precision=lax.Precision.HIGHEST,
        preferred_element_type=F32).astype(BF16)


def _prep_w_in(w_in):
    depth = w_in.shape[0]
    rows = 256
    return pl.pallas_call(
        _prep_w_in_body,
        grid=(depth, D_MODEL // rows),
        in_specs=[pl.BlockSpec((1, rows, w_in.shape[-1]), lambda l, i: (l, i, 0))],
        out_specs=pl.BlockSpec((1, rows, IN_EXT), lambda l, i: (l, i, 0)),
        out_shape=jax.ShapeDtypeStruct((depth, D_MODEL, IN_EXT), BF16),
        scratch_shapes=[pltpu.VMEM((D_FFT, 2 * D_FFT), F32)],
        compiler_params=_params(("arbitrary", "arbitrary")),
        name="prep_w_in",
    )(w_in)


def _prep_w_out_body(w_ref, g_ref, o_ref):
    o_ref[0] = (w_ref[0] * g_ref[0]).astype(BF16)


def _prep_w_out(w_out, g_rows):
    depth, d, _ = w_out.shape
    return pl.pallas_call(
        _prep_w_out_body,
        grid=(depth,),
        in_specs=[pl.BlockSpec((1, d, d), lambda l: (l, 0, 0)),
                  pl.BlockSpec((1, d, 1), lambda l: (l, 0, 0))],
        out_specs=pl.BlockSpec((1, d, d), lambda l: (l, 0, 0)),
        out_shape=jax.ShapeDtypeStruct((depth, d, d), BF16),
        compiler_params=_params(("arbitrary",)),
        name="prep_w_out",
    )(w_out, g_rows.reshape(depth, d, 1))


def _prep_gate_up_body(g_ref, u_ref, o_ref):
    o_ref[0, :, :MXU_COLS] = g_ref[0].astype(BF16)
    o_ref[0, :, MXU_COLS:] = u_ref[0].astype(BF16)


def _prep_gate_up(w_gate, w_up):
    depth, d, ff = w_gate.shape
    src = pl.BlockSpec((1, d, MXU_COLS), lambda l, j: (l, 0, j))
    return pl.pallas_call(
        _prep_gate_up_body,
        grid=(depth, ff // MXU_COLS),
        in_specs=[src, src],
        out_specs=pl.BlockSpec((1, d, 2 * MXU_COLS), lambda l, j: (l, 0, j)),
        out_shape=jax.ShapeDtypeStruct((depth, d, 2 * ff), BF16),
        compiler_params=_params(("arbitrary", "arbitrary")),
        name="prep_gate_up",
    )(w_gate, w_up)


def _mod_body(c_ref, w_ref, b_ref, o_ref):
    c = c_ref[...]
    act = c * jax.nn.sigmoid(c)
    o_ref[0] = jnp.dot(act, w_ref[0], precision=lax.Precision.HIGHEST,
                       preferred_element_type=F32) + b_ref[0]


def _modulation(c_all, w_ada, b_ada):
    depth = w_ada.shape[0]
    rows = c_all.shape[0]
    return pl.pallas_call(
        _mod_body,
        grid=(depth, N_MOD),
        in_specs=[
            pl.BlockSpec((rows, D_MODEL), lambda l, j: (0, 0)),
            pl.BlockSpec((1, D_MODEL, D_MODEL), lambda l, j: (l, 0, j)),
            pl.BlockSpec((1, 1, D_MODEL), lambda l, j: (l, 0, j)),
        ],
        out_specs=pl.BlockSpec((1, rows, D_MODEL), lambda l, j: (l, 0, j)),
        out_shape=jax.ShapeDtypeStruct((depth, rows, N_MOD * D_MODEL), F32),
        compiler_params=_params(("arbitrary", "arbitrary")),
        name="adaln_modulation",
    )(c_all, w_ada, b_ada.reshape(depth, 1, N_MOD * D_MODEL))


def _mix_in_body(x_ref, mod_ref, g_ref, w_ref, bg_ref, u_ref, a_ref):
    d = D_MODEL
    sh = mod_ref[:, 0 * d:1 * d]
    sc = mod_ref[:, 1 * d:2 * d]
    gain = g_ref[...] * (1.0 + sc)
    ts = x_ref.shape[1] // MIX_IN_SUB_TILES
    for sub in range(MIX_IN_SUB_TILES):
        rows = slice(sub * ts, (sub + 1) * ts)
        h = _rms(x_ref[0, rows, :], gain) + sh
        z = jnp.dot(h.astype(BF16), w_ref[...], preferred_element_type=F32)
        bg_ref[0, rows, :] = z[:, :D_CONV]
        u_ref[0, rows, :] = z[:, D_CONV:2 * D_CONV] * z[:, 2 * D_CONV:3 * D_CONV]
        a_ref[0, 0, rows, :] = z[:, 3 * D_CONV:3 * D_CONV + D_FFT].astype(BF16)
        a_ref[0, 1, rows, :] = z[:, 3 * D_CONV + D_FFT:].astype(BF16)


def _mix_in(x, mod_all, g_pre, w_in_ext, layer, row0):
    bsz, seq, d = x.shape
    tm = MIX_IN_TILE
    return pl.pallas_call(
        _mix_in_body,
        grid=(bsz, seq // tm),
        in_specs=[
            pl.BlockSpec((1, tm, d), lambda b, i: (b, i, 0)),
            pl.BlockSpec((None, None, 1, N_MOD * d), lambda b, i: (layer, row0 + b, 0, 0)),
            _resident((None, 1, d), lambda b, i: (layer, 0, 0)),
            _resident((None, d, IN_EXT), lambda b, i: (layer, 0, 0)),
        ],
        out_specs=[
            pl.BlockSpec((1, tm, D_CONV), lambda b, i: (b, i, 0)),
            pl.BlockSpec((1, tm, D_CONV), lambda b, i: (b, i, 0)),
            pl.BlockSpec((1, 2, tm, D_FFT), lambda b, i: (b, 0, i, 0)),
        ],
        out_shape=[
            jax.ShapeDtypeStruct((bsz, seq, D_CONV), F32),
            jax.ShapeDtypeStruct((bsz, seq, D_CONV), F32),
            jax.ShapeDtypeStruct((bsz, 2, seq, D_FFT), BF16),
        ],
        compiler_params=_params(("arbitrary", "arbitrary")),
        name="mix_in",
    )(x, mod_all, g_pre, w_in_ext)


def _seq_fft_body(a_ref, ka_ref, db_ref, o_ref, y_ref, *, n1, n2):
    rows = FFT_ROWS
    half = n1 * rows
    for j in range(n2 // rows):
        r0 = j * rows
        x = jnp.concatenate([a_ref[0, 0, :, r0:r0 + rows, :].reshape(half, D_FFT),
                             a_ref[0, 1, :, r0:r0 + rows, :].reshape(half, D_FFT)], axis=0)
        y = jnp.dot(ka_ref[...], x, preferred_element_type=F32)
        y_ref[:, 0, r0:r0 + rows, :] = y[:half].astype(BF16).reshape(n1, rows, D_FFT)
        y_ref[:, 1, r0:r0 + rows, :] = y[half:].astype(BF16).reshape(n1, rows, D_FFT)

    for k1 in range(n1):
        f = jnp.dot(db_ref[k1], y_ref[k1].reshape(2 * n2, D_FFT), preferred_element_type=F32)
        o_ref[0, k1] = _rms(f).astype(BF16)


def _seq_fft(tables, a):
    ka, db, _ = tables
    bsz, _, seq, _ = a.shape
    n1, n2 = FFT_N1, seq // FFT_N1
    a5 = a.reshape(bsz, 2, n1, n2, D_FFT)
    return pl.pallas_call(
        functools.partial(_seq_fft_body, n1=n1, n2=n2),
        grid=(bsz,),
        in_specs=[
            pl.BlockSpec((1, 2, n1, n2, D_FFT), lambda b: (b, 0, 0, 0, 0)),
            _resident(ka.shape, lambda b: (0, 0)),
            _resident(db.shape, lambda b: (0, 0, 0)),
        ],
        out_specs=pl.BlockSpec((1, n1, n2, D_FFT), lambda b: (b, 0, 0, 0)),
        out_shape=jax.ShapeDtypeStruct((bsz, n1, n2, D_FFT), BF16),
        scratch_shapes=[pltpu.VMEM((n1, 2, n2, D_FFT), BF16)],
        compiler_params=_params(("arbitrary",)),
        name="seq_fft",
    )(a5, ka, db)


def _mix_out_body(x_ref, bg_ref, u_ref, up_ref, un_ref, fn_ref, perm_ref, mod_ref, cw_ref, cb_ref,
                  gpm_ref, wo_ref, gpf_ref, gpo_ref, wgu_ref, wd_ref, o_ref):
    d = D_MODEL
    tm = u_ref.shape[1]
    ts = tm // SUB_TILES
    i = pl.program_id(1)
    last = pl.num_programs(1) - 1

    mix_gain = mod_ref[:, 2 * d:3 * d] * gpm_ref[...]
    ffn_in_gain = gpf_ref[...] * (1.0 + mod_ref[:, 4 * d:5 * d])
    ffn_out_gain = mod_ref[:, 5 * d:6 * d] * gpo_ref[...]

    def rows_of(sub):
        return slice(sub * ts, (sub + 1) * ts)

    def conv_branch(sub):
        r0 = sub * ts
        u = u_ref[0, rows_of(sub), :].astype(F32)
        if sub == 0:
            prev = jnp.where(i > 0, up_ref[0, CONV_HALO - 1:CONV_HALO, :].astype(F32), 0.0)
        else:
            prev = u_ref[0, r0 - 1:r0, :].astype(F32)
        if sub == SUB_TILES - 1:
            nxt = jnp.where(i < last, un_ref[0, 0:1, :].astype(F32), 0.0)
        else:
            nxt = u_ref[0, r0 + ts:r0 + ts + 1, :].astype(F32)
        row = lax.broadcasted_iota(jnp.int32, u.shape, 0)
        u_m1 = jnp.where(row == 0, prev, pltpu.roll(u, 1, 0))
        u_p1 = jnp.where(row == ts - 1, nxt, pltpu.roll(u, ts - 1, 0))
        conv = u_m1 * cw_ref[0:1, :] + u * cw_ref[1:2, :] + u_p1 * cw_ref[2:3, :] + cb_ref[...]
        conv_n = _rms(bg_ref[0, rows_of(sub), :] * conv).astype(BF16)
        q = ts // FFT_N1
        fn = fn_ref[0, :, sub * q:(sub + 1) * q, :].reshape(ts, D_FFT)
        fn = jnp.dot(perm_ref[...], fn, preferred_element_type=F32).astype(BF16)
        return jnp.concatenate([conv_n, fn], axis=-1)

    def out_proj(merged):
        return jnp.dot(merged, wo_ref[...], preferred_element_type=F32)

    def residual_and_ffn_input(sub, o):
        x1 = x_ref[0, rows_of(sub), :] + _rms(o, mix_gain)
        h = _rms(x1, ffn_in_gain) + mod_ref[:, 3 * d:4 * d]
        return x1, h.astype(BF16)

    def ffn_chunk(hb, ff, c0):
        c1 = min(c0 + FF_CHUNK, D_FF)
        gu = jnp.dot(hb, wgu_ref[:, 2 * c0:2 * c1], preferred_element_type=F32)
        acts = []
        for b0 in range(0, 2 * (c1 - c0), 2 * MXU_COLS):
            gate = gu[:, b0:b0 + MXU_COLS]
            up = gu[:, b0 + MXU_COLS:b0 + 2 * MXU_COLS]
            half = 0.5 * gate
            acts.append((half * (1.0 + jnp.tanh(half)) * up).astype(BF16))
        act = jnp.concatenate(acts, axis=-1)
        part = jnp.dot(act, wd_ref[c0:c1, :], preferred_element_type=F32)
        return part if ff is None else ff + part

    def finish(sub, x1, ff):
        o_ref[0, rows_of(sub), :] = x1 + _rms(ff, ffn_out_gain)

    chunks = list(range(0, D_FF, FF_CHUNK))
    x1, hb = residual_and_ffn_input(0, out_proj(conv_branch(0)))
    pending = None
    for sub in range(SUB_TILES):
        more = sub + 1 < SUB_TILES
        if more:
            o_next = out_proj(conv_branch(sub + 1))
        ff = ffn_chunk(hb, None, chunks[0])
        if pending is not None:
            finish(*pending)
        if more:
            x1_next, hb_next = residual_and_ffn_input(sub + 1, o_next)
        for c0 in chunks[1:]:
            ff = ffn_chunk(hb, ff, c0)
        pending = (sub, x1, ff)
        if more:
            x1, hb = x1_next, hb_next
    finish(*pending)


def _mix_out(x, bg, u, fn, perm, mod_all, conv_w, conv_b, g_post_mix, w_out, g_pre_ffn,
             g_post_ffn, w_gate_up, w_down, layer, row0):
    bsz, seq, d = x.shape
    tm = MIX_OUT_TILE
    halo_per_tile = tm // CONV_HALO
    n_halo = seq // CONV_HALO
    tok = lambda b, i: (b, i, 0)
    lay = lambda b, i: (layer, 0, 0)
    return pl.pallas_call(
        _mix_out_body,
        grid=(bsz, seq // tm),
        in_specs=[
            pl.BlockSpec((1, tm, d), tok),
            pl.BlockSpec((1, tm, D_CONV), tok),
            pl.BlockSpec((1, tm, D_CONV), tok),
            pl.BlockSpec((1, CONV_HALO, D_CONV),
                         lambda b, i: (b, jnp.maximum(i * halo_per_tile - 1, 0), 0)),
            pl.BlockSpec((1, CONV_HALO, D_CONV),
                         lambda b, i: (b, jnp.minimum((i + 1) * halo_per_tile, n_halo - 1), 0)),
            pl.BlockSpec((1, FFT_N1, tm // FFT_N1, D_FFT), lambda b, i: (b, 0, i, 0)),
            _resident(perm.shape, lambda b, i: (0, 0)),
            pl.BlockSpec((None, None, 1, N_MOD * d), lambda b, i: (layer, row0 + b, 0, 0)),
            _resident((None, 3, D_CONV), lay),
            _resident((None, 1, D_CONV), lay),
            _resident((None, 1, d), lay),
            _resident((None, d, d), lay),
            _resident((None, 1, d), lay),
            _resident((None, 1, d), lay),
            _resident((None, d, 2 * D_FF), lay),
            _resident((None, D_FF, d), lay),
        ],
        out_specs=pl.BlockSpec((1, tm, d), tok),
        out_shape=jax.ShapeDtypeStruct((bsz, seq, d), F32),
        compiler_params=_params(("arbitrary", "arbitrary")),
        name="mix_out",
    )(x, bg, u, u, u, fn, perm, mod_all, conv_w, conv_b, g_post_mix, w_out, g_pre_ffn,
      g_post_ffn, w_gate_up, w_down)


def kernel(x_prompt, x_sample, c_prompt, c_sample, w_ada, b_ada, g_pre_mix, g_post_mix, w_in,
           conv_w, conv_b, g_conv, g_fft, w_out, g_pre_ffn, g_post_ffn, w_gate, w_up, w_down):
    depth = w_in.shape[0]
    row3 = lambda g: g.reshape(depth, 1, g.shape[-1])

    w_in_ext = _prep_w_in(w_in)
    w_out_b = _prep_w_out(w_out, jnp.concatenate([g_conv, g_fft], axis=-1))
    w_gate_up = _prep_gate_up(w_gate, w_up)
    w_down_b = w_down.astype(BF16)

    c_all = jnp.concatenate([c_prompt, c_sample], axis=0)
    mod_all = _modulation(c_all, w_ada, b_ada)
    mod_all = mod_all.reshape(depth, c_all.shape[0], 1, N_MOD * D_MODEL)

    g_pre_mix3, g_post_mix3 = row3(g_pre_mix), row3(g_post_mix)
    conv_b3 = row3(conv_b)
    g_pre_ffn3, g_post_ffn3 = row3(g_pre_ffn), row3(g_post_ffn)

    outs = []
    for x, row0 in ((x_prompt, 0), (x_sample, c_prompt.shape[0])):
        tables = _fft_tables(x.shape[1])
        for layer in range(depth):
            bg, u, a = _mix_in(x, mod_all, g_pre_mix3, w_in_ext, layer, row0)
            fn = _seq_fft(tables, a)
            x = _mix_out(x, bg, u, fn, tables[2], mod_all, conv_w, conv_b3, g_post_mix3, w_out_b,
                         g_pre_ffn3, g_post_ffn3, w_gate_up, w_down_b, layer, row0)
        outs.append(x)
    return tuple(outs)
```

```python
import functools
import math

import jax
import jax.numpy as jnp
from jax import lax
from jax.experimental import pallas as pl
from jax.experimental.pallas import tpu as pltpu

D_MODEL = 1024
D_CONV = 512
D_FFT = 512
HEAD_DIM = 64
N_FFT_GROUPS = D_FFT // HEAD_DIM
D_FF = 2816
N_MOD = 6
EPS = 1e-6
IN_EXT = 3 * D_CONV + 2 * D_FFT

F32 = jnp.float32
BF16 = jnp.bfloat16

V7X_VMEM_BYTES = 64 * 1024 * 1024
VMEM_LIMIT = V7X_VMEM_BYTES - 8 * 1024 * 1024

MIX_IN_TILE = 2048
MIX_IN_SUB_TILES = 2
MIX_OUT_TILE = 1024
SUB_TILES = 4
BF16_SUBLANES = 16
MXU_COLS = 256
FFT_N1 = BF16_SUBLANES
FFT_ROWS = BF16_SUBLANES
FF_CHUNK = 1536
CONV_HALO = BF16_SUBLANES


def _rms(v, g=None):
    y = v * lax.rsqrt(jnp.mean(v * v, axis=-1, keepdims=True) + EPS)
    return y if g is None else y * g


def _resident(block_shape, index_map):
    return pl.BlockSpec(block_shape, index_map, pipeline_mode=pl.Buffered(1))


def _params(sem):
    return pltpu.CompilerParams(dimension_semantics=sem, vmem_limit_bytes=VMEM_LIMIT)


def _fft_tables_body(ka_ref, db_ref, perm_ref, *, n1, n2):
    half = n1 * FFT_ROWS
    i = lax.broadcasted_iota(jnp.int32, (2 * half, 2 * half), 0)
    j = lax.broadcasted_iota(jnp.int32, (2 * half, 2 * half), 1)
    k1 = (i & (half - 1)) // FFT_ROWS
    s1 = (j & (half - 1)) // FFT_ROWS
    same_r = (i & (FFT_ROWS - 1)) == (j & (FFT_ROWS - 1))
    ang = ((k1 * s1) & (n1 - 1)).astype(F32) * (2.0 * math.pi / n1)
    pi_, pj = i >= half, j >= half
    val = jnp.where(pi_ == pj, jnp.cos(ang), jnp.where(pj, -jnp.sin(ang), jnp.sin(ang)))
    ka_ref[...] = jnp.where(same_r, val, 0.0).astype(BF16)

    seq = n1 * n2
    k2 = lax.broadcasted_iota(jnp.int32, (n2, n2), 0)
    s2 = lax.broadcasted_iota(jnp.int32, (n2, n2), 1)
    for k1 in range(n1):
        ang = (((k1 + n1 * k2) * s2) & (seq - 1)).astype(F32) * (2.0 * math.pi / seq)
        db_ref[k1, :, :n2] = jnp.cos(ang).astype(BF16)
        db_ref[k1, :, n2:] = (-jnp.sin(ang)).astype(BF16)

    ts = perm_ref.shape[0]
    tok = lax.broadcasted_iota(jnp.int32, (ts, ts), 0)
    src = lax.broadcasted_iota(jnp.int32, (ts, ts), 1)
    perm_ref[...] = (src == (tok & (n1 - 1)) * (ts // n1) + tok // n1).astype(BF16)


def _fft_tables(seq):
    n1, n2 = FFT_N1, seq // FFT_N1
    half = n1 * FFT_ROWS
    ts = MIX_OUT_TILE // SUB_TILES
    return pl.pallas_call(
        functools.partial(_fft_tables_body, n1=n1, n2=n2),
        out_shape=[
            jax.ShapeDtypeStruct((2 * half, 2 * half), BF16),
            jax.ShapeDtypeStruct((n1, n2, 2 * n2), BF16),
            jax.ShapeDtypeStruct((ts, ts), BF16),
        ],
        compiler_params=pltpu.CompilerParams(vmem_limit_bytes=VMEM_LIMIT),
        name=f"fft_tables_{seq}",
    )()


def _prep_w_in_body(w_ref, o_ref, tab_ref):
    @pl.when((pl.program_id(0) == 0) & (pl.program_id(1) == 0))
    def _():
        r = lax.broadcasted_iota(jnp.int32, (D_FFT, 2 * D_FFT), 0)
        q = lax.broadcasted_iota(jnp.int32, (D_FFT, 2 * D_FFT), 1)
        same_group = (r // HEAD_DIM) == ((q & (D_FFT - 1)) // HEAD_DIM)
        cm = ((r & (HEAD_DIM - 1)) * (q & (HEAD_DIM - 1))) & (HEAD_DIM - 1)
        ang = cm.astype(F32) * (2.0 * math.pi / HEAD_DIM)
        tab_ref[...] = jnp.where(same_group, jnp.where(q < D_FFT, jnp.cos(ang), jnp.sin(ang)), 0.0)

    o_ref[0, :, :3 * D_CONV] = w_ref[0, :, :3 * D_CONV].astype(BF16)
    o_ref[0, :, 3 * D_CONV:] = jnp.dot(
        w_ref[0, :, 3 * D_CONV:], tab_ref[...], precision=lax.Precision.HIGHEST,
        preferred_element_type=F32).astype(BF16)


def _prep_w_in(w_in):
    depth = w_in.shape[0]
    rows = 256
    return pl.pallas_call(
        _prep_w_in_body,
        grid=(depth, D_MODEL // rows),
        in_specs=[pl.BlockSpec((1, rows, w_in.shape[-1]), lambda l, i: (l, i, 0))],
        out_specs=pl.BlockSpec((1, rows, IN_EXT), lambda l, i: (l, i, 0)),
        out_shape=jax.ShapeDtypeStruct((depth, D_MODEL, IN_EXT), BF16),
        scratch_shapes=[pltpu.VMEM((D_FFT, 2 * D_FFT), F32)],
        compiler_params=_params(("arbitrary", "arbitrary")),
        name="prep_w_in",
    )(w_in)


def _prep_w_out_body(w_ref, g_ref, o_ref):
    o_ref[0] = (w_ref[0] * g_ref[0]).astype(BF16)


def _prep_w_out(w_out, g_rows):
    depth, d, _ = w_out.shape
    return pl.pallas_call(
        _prep_w_out_body,
        grid=(depth,),
        in_specs=[pl.BlockSpec((1, d, d), lambda l: (l, 0, 0)),
                  pl.BlockSpec((1, d, 1), lambda l: (l, 0, 0))],
        out_specs=pl.BlockSpec((1, d, d), lambda l: (l, 0, 0)),
        out_shape=jax.ShapeDtypeStruct((depth, d, d), BF16),
        compiler_params=_params(("arbitrary",)),
        name="prep_w_out",
    )(w_out, g_rows.reshape(depth, d, 1))


def _prep_gate_up_body(g_ref, u_ref, o_ref):
    for j in range(g_ref.shape[-1] // MXU_COLS):
        src = slice(j * MXU_COLS, (j + 1) * MXU_COLS)
        o_ref[0, :, 2 * j * MXU_COLS:(2 * j + 1) * MXU_COLS] = g_ref[0, :, src].astype(BF16)
        o_ref[0, :, (2 * j + 1) * MXU_COLS:(2 * j + 2) * MXU_COLS] = u_ref[0, :, src].astype(BF16)


def _prep_gate_up(w_gate, w_up):
    depth, d, ff = w_gate.shape
    rows = 256
    src = pl.BlockSpec((1, rows, ff), lambda l, i: (l, i, 0))
    return pl.pallas_call(
        _prep_gate_up_body,
        grid=(depth, d // rows),
        in_specs=[src, src],
        out_specs=pl.BlockSpec((1, rows, 2 * ff), lambda l, i: (l, i, 0)),
        out_shape=jax.ShapeDtypeStruct((depth, d, 2 * ff), BF16),
        compiler_params=_params(("arbitrary", "arbitrary")),
        name="prep_gate_up",
    )(w_gate, w_up)


def _mod_body(c_ref, w_ref, b_ref, o_ref):
    c = c_ref[...]
    act = c * jax.nn.sigmoid(c)
    o_ref[0] = jnp.dot(act.astype(BF16), w_ref[0].astype(BF16),
                       preferred_element_type=F32) + b_ref[0]


def _modulation(c_all, w_ada, b_ada):
    depth = w_ada.shape[0]
    rows = c_all.shape[0]
    return pl.pallas_call(
        _mod_body,
        grid=(depth, N_MOD),
        in_specs=[
            pl.BlockSpec((rows, D_MODEL), lambda l, j: (0, 0)),
            pl.BlockSpec((1, D_MODEL, D_MODEL), lambda l, j: (l, 0, j)),
            pl.BlockSpec((1, 1, D_MODEL), lambda l, j: (l, 0, j)),
        ],
        out_specs=pl.BlockSpec((1, rows, D_MODEL), lambda l, j: (l, 0, j)),
        out_shape=jax.ShapeDtypeStruct((depth, rows, N_MOD * D_MODEL), F32),
        compiler_params=_params(("arbitrary", "arbitrary")),
        name="adaln_modulation",
    )(c_all, w_ada, b_ada.reshape(depth, 1, N_MOD * D_MODEL))


def _mix_in_body(x_ref, mod_ref, g_ref, w_ref, bg_ref, u_ref, a_ref):
    d = D_MODEL
    sh = mod_ref[:, 0 * d:1 * d]
    sc = mod_ref[:, 1 * d:2 * d]
    gain = g_ref[...] * (1.0 + sc)
    ts = x_ref.shape[1] // MIX_IN_SUB_TILES
    for sub in range(MIX_IN_SUB_TILES):
        rows = slice(sub * ts, (sub + 1) * ts)
        h = _rms(x_ref[0, rows, :], gain) + sh
        z = jnp.dot(h.astype(BF16), w_ref[...], preferred_element_type=F32)
        bg_ref[0, rows, :] = z[:, :D_CONV]
        u_ref[0, rows, :] = z[:, D_CONV:2 * D_CONV] * z[:, 2 * D_CONV:3 * D_CONV]
        a_ref[0, 0, rows, :] = z[:, 3 * D_CONV:3 * D_CONV + D_FFT].astype(BF16)
        a_ref[0, 1, rows, :] = z[:, 3 * D_CONV + D_FFT:].astype(BF16)


def _mix_in(x, mod_all, g_pre, w_in_ext, layer, row0):
    bsz, seq, d = x.shape
    tm = MIX_IN_TILE
    return pl.pallas_call(
        _mix_in_body,
        grid=(bsz, seq // tm),
        in_specs=[
            pl.BlockSpec((1, tm, d), lambda b, i: (b, i, 0)),
            pl.BlockSpec((None, None, 1, N_MOD * d), lambda b, i: (layer, row0 + b, 0, 0)),
            _resident((None, 1, d), lambda b, i: (layer, 0, 0)),
            _resident((None, d, IN_EXT), lambda b, i: (layer, 0, 0)),
        ],
        out_specs=[
            pl.BlockSpec((1, tm, D_CONV), lambda b, i: (b, i, 0)),
            pl.BlockSpec((1, tm, D_CONV), lambda b, i: (b, i, 0)),
            pl.BlockSpec((1, 2, tm, D_FFT), lambda b, i: (b, 0, i, 0)),
        ],
        out_shape=[
            jax.ShapeDtypeStruct((bsz, seq, D_CONV), F32),
            jax.ShapeDtypeStruct((bsz, seq, D_CONV), F32),
            jax.ShapeDtypeStruct((bsz, 2, seq, D_FFT), BF16),
        ],
        compiler_params=_params(("arbitrary", "arbitrary")),
        name="mix_in",
    )(x, mod_all, g_pre, w_in_ext)


def _seq_fft_body(a_ref, ka_ref, db_ref, o_ref, y_ref, *, n1, n2):
    rows = FFT_ROWS
    half = n1 * rows
    for j in range(n2 // rows):
        r0 = j * rows
        x = jnp.concatenate([a_ref[0, 0, :, r0:r0 + rows, :].reshape(half, D_FFT),
                             a_ref[0, 1, :, r0:r0 + rows, :].reshape(half, D_FFT)], axis=0)
        y = jnp.dot(ka_ref[...], x, preferred_element_type=F32)
        y_ref[:, 0, r0:r0 + rows, :] = y[:half].astype(BF16).reshape(n1, rows, D_FFT)
        y_ref[:, 1, r0:r0 + rows, :] = y[half:].astype(BF16).reshape(n1, rows, D_FFT)

    for k1 in range(n1):
        f = jnp.dot(db_ref[k1], y_ref[k1].reshape(2 * n2, D_FFT), preferred_element_type=F32)
        o_ref[0, k1] = _rms(f).astype(BF16)


def _seq_fft(tables, a):
    ka, db, _ = tables
    bsz, _, seq, _ = a.shape
    n1, n2 = FFT_N1, seq // FFT_N1
    a5 = a.reshape(bsz, 2, n1, n2, D_FFT)
    return pl.pallas_call(
        functools.partial(_seq_fft_body, n1=n1, n2=n2),
        grid=(bsz,),
        in_specs=[
            pl.BlockSpec((1, 2, n1, n2, D_FFT), lambda b: (b, 0, 0, 0, 0)),
            _resident(ka.shape, lambda b: (0, 0)),
            _resident(db.shape, lambda b: (0, 0, 0)),
        ],
        out_specs=pl.BlockSpec((1, n1, n2, D_FFT), lambda b: (b, 0, 0, 0)),
        out_shape=jax.ShapeDtypeStruct((bsz, n1, n2, D_FFT), BF16),
        scratch_shapes=[pltpu.VMEM((n1, 2, n2, D_FFT), BF16)],
        compiler_params=_params(("arbitrary",)),
        name="seq_fft",
    )(a5, ka, db)


def _mix_out_body(x_ref, bg_ref, u_ref, up_ref, un_ref, fn_ref, perm_ref, mod_ref, cw_ref, cb_ref,
                  gpm_ref, wo_ref, gpf_ref, gpo_ref, wgu_ref, wd_ref, o_ref):
    d = D_MODEL
    tm = u_ref.shape[1]
    ts = tm // SUB_TILES
    i = pl.program_id(1)
    last = pl.num_programs(1) - 1

    mix_gain = mod_ref[:, 2 * d:3 * d] * gpm_ref[...]
    ffn_in_gain = gpf_ref[...] * (1.0 + mod_ref[:, 4 * d:5 * d])
    ffn_out_gain = mod_ref[:, 5 * d:6 * d] * gpo_ref[...]

    def rows_of(sub):
        return slice(sub * ts, (sub + 1) * ts)

    def conv_branch(sub):
        r0 = sub * ts
        u = u_ref[0, rows_of(sub), :].astype(F32)
        if sub == 0:
            prev = jnp.where(i > 0, up_ref[0, CONV_HALO - 1:CONV_HALO, :].astype(F32), 0.0)
        else:
            prev = u_ref[0, r0 - 1:r0, :].astype(F32)
        if sub == SUB_TILES - 1:
            nxt = jnp.where(i < last, un_ref[0, 0:1, :].astype(F32), 0.0)
        else:
            nxt = u_ref[0, r0 + ts:r0 + ts + 1, :].astype(F32)
        row = lax.broadcasted_iota(jnp.int32, u.shape, 0)
        u_m1 = jnp.where(row == 0, prev, pltpu.roll(u, 1, 0))
        u_p1 = jnp.where(row == ts - 1, nxt, pltpu.roll(u, ts - 1, 0))
        conv = u_m1 * cw_ref[0:1, :] + u * cw_ref[1:2, :] + u_p1 * cw_ref[2:3, :] + cb_ref[...]
        conv_n = _rms(bg_ref[0, rows_of(sub), :] * conv).astype(BF16)
        q = ts // FFT_N1
        fn = fn_ref[0, :, sub * q:(sub + 1) * q, :].reshape(ts, D_FFT)
        fn = jnp.dot(perm_ref[...], fn, preferred_element_type=F32).astype(BF16)
        return jnp.concatenate([conv_n, fn], axis=-1)

    def out_proj(merged):
        return jnp.dot(merged, wo_ref[...], preferred_element_type=F32)

    def residual_and_ffn_input(sub, o):
        x1 = x_ref[0, rows_of(sub), :] + _rms(o, mix_gain)
        h = _rms(x1, ffn_in_gain) + mod_ref[:, 3 * d:4 * d]
        return x1, h.astype(BF16)

    def ffn_chunk(hb, ff, c0):
        c1 = min(c0 + FF_CHUNK, D_FF)
        gu = jnp.dot(hb, wgu_ref[:, 2 * c0:2 * c1], preferred_element_type=F32)
        acts = []
        for b0 in range(0, 2 * (c1 - c0), 2 * MXU_COLS):
            gate = gu[:, b0:b0 + MXU_COLS]
            up = gu[:, b0 + MXU_COLS:b0 + 2 * MXU_COLS]
            half = 0.5 * gate
            acts.append((half * (1.0 + jnp.tanh(half)) * up).astype(BF16))
        act = jnp.concatenate(acts, axis=-1)
        part = jnp.dot(act, wd_ref[c0:c1, :], preferred_element_type=F32)
        return part if ff is None else ff + part

    def finish(sub, x1, ff):
        o_ref[0, rows_of(sub), :] = x1 + _rms(ff, ffn_out_gain)

    chunks = list(range(0, D_FF, FF_CHUNK))
    x1, hb = residual_and_ffn_input(0, out_proj(conv_branch(0)))
    pending = None
    for sub in range(SUB_TILES):
        more = sub + 1 < SUB_TILES
        if more:
            o_next = out_proj(conv_branch(sub + 1))
        ff = ffn_chunk(hb, None, chunks[0])
        if pending is not None:
            finish(*pending)
        if more:
            x1_next, hb_next = residual_and_ffn_input(sub + 1, o_next)
        for c0 in chunks[1:]:
            ff = ffn_chunk(hb, ff, c0)
        pending = (sub, x1, ff)
        if more:
            x1, hb = x1_next, hb_next
    finish(*pending)


def _mix_out(x, bg, u, fn, perm, mod_all, conv_w, conv_b, g_post_mix, w_out, g_pre_ffn,
             g_post_ffn, w_gate_up, w_down, layer, row0):
    bsz, seq, d = x.shape
    tm = MIX_OUT_TILE
    halo_per_tile = tm // CONV_HALO
    n_halo = seq // CONV_HALO
    tok = lambda b, i: (b, i, 0)
    lay = lambda b, i: (layer, 0, 0)
    return pl.pallas_call(
        _mix_out_body,
        grid=(bsz, seq // tm),
        in_specs=[
            pl.BlockSpec((1, tm, d), tok),
            pl.BlockSpec((1, tm, D_CONV), tok),
            pl.BlockSpec((1, tm, D_CONV), tok),
            pl.BlockSpec((1, CONV_HALO, D_CONV),
                         lambda b, i: (b, jnp.maximum(i * halo_per_tile - 1, 0), 0)),
            pl.BlockSpec((1, CONV_HALO, D_CONV),
                         lambda b, i: (b, jnp.minimum((i + 1) * halo_per_tile, n_halo - 1), 0)),
            pl.BlockSpec((1, FFT_N1, tm // FFT_N1, D_FFT), lambda b, i: (b, 0, i, 0)),
            _resident(perm.shape, lambda b, i: (0, 0)),
            pl.BlockSpec((None, None, 1, N_MOD * d), lambda b, i: (layer, row0 + b, 0, 0)),
            _resident((None, 3, D_CONV), lay),
            _resident((None, 1, D_CONV), lay),
            _resident((None, 1, d), lay),
            _resident((None, d, d), lay),
            _resident((None, 1, d), lay),
            _resident((None, 1, d), lay),
            _resident((None, d, 2 * D_FF), lay),
            _resident((None, D_FF, d), lay),
        ],
        out_specs=pl.BlockSpec((1, tm, d), tok),
        out_shape=jax.ShapeDtypeStruct((bsz, seq, d), F32),
        compiler_params=_params(("arbitrary", "arbitrary")),
        name="mix_out",
    )(x, bg, u, u, u, fn, perm, mod_all, conv_w, conv_b, g_post_mix, w_out, g_pre_ffn,
      g_post_ffn, w_gate_up, w_down)


def kernel(x_prompt, x_sample, c_prompt, c_sample, w_ada, b_ada, g_pre_mix, g_post_mix, w_in,
           conv_w, conv_b, g_conv, g_fft, w_out, g_pre_ffn, g_post_ffn, w_gate, w_up, w_down):
    depth = w_in.shape[0]
    row3 = lambda g: g.reshape(depth, 1, g.shape[-1])

    w_in_ext = _prep_w_in(w_in)
    w_out_b = _prep_w_out(w_out, jnp.concatenate([g_conv, g_fft], axis=-1))
    w_gate_up = _prep_gate_up(w_gate, w_up)
    w_down_b = w_down.astype(BF16)

    c_all = jnp.concatenate([c_prompt, c_sample], axis=0)
    mod_all = _modulation(c_all, w_ada, b_ada)
    mod_all = mod_all.reshape(depth, c_all.shape[0], 1, N_MOD * D_MODEL)

    g_pre_mix3, g_post_mix3 = row3(g_pre_mix), row3(g_post_mix)
    conv_b3 = row3(conv_b)
    g_pre_ffn3, g_post_ffn3 = row3(g_pre_ffn), row3(g_post_ffn)

    outs = []
    for x, row0 in ((x_prompt, 0), (x_sample, c_prompt.shape[0])):
        tables = _fft_tables(x.shape[1])
        for layer in range(depth):
            bg, u, a = _mix_in(x, mod_all, g_pre_mix3, w_in_ext, layer, row0)
            fn = _seq_fft(tables, a)
            x = _mix_out(x, bg, u, fn, tables[2], mod_all, conv_w, conv_b3, g_post_mix3, w_out_b,
                         g_pre_ffn3, g_post_ffn3, w_gate_up, w_down_b, layer, row0)
        outs.append(x)
    return tuple(outs)
```

```python
import functools
import math

import jax
import jax.numpy as jnp
from jax import lax
from jax.experimental import pallas as pl
from jax.experimental.pallas import tpu as pltpu

D_MODEL = 1024
D_CONV = 512
D_FFT = 512
HEAD_DIM = 64
N_FFT_GROUPS = D_FFT // HEAD_DIM
D_FF = 2816
N_MOD = 6
EPS = 1e-6
IN_EXT = 3 * D_CONV + 2 * D_FFT

F32 = jnp.float32
BF16 = jnp.bfloat16

V7X_VMEM_BYTES = 64 * 1024 * 1024
VMEM_LIMIT = V7X_VMEM_BYTES - 8 * 1024 * 1024

MIX_IN_TILE = 2048
MIX_IN_SUB_TILES = 4
MIX_OUT_TILE = 1024
SUB_TILES = 4
BF16_SUBLANES = 16
MXU_COLS = 256
FFT_N1 = BF16_SUBLANES
FFT_ROWS = BF16_SUBLANES
FF_CHUNK = 1536
CONV_HALO = BF16_SUBLANES


def _rms(v, g=None):
    y = v * lax.rsqrt(jnp.mean(v * v, axis=-1, keepdims=True) + EPS)
    return y if g is None else y * g


def _resident(block_shape, index_map):
    return pl.BlockSpec(block_shape, index_map, pipeline_mode=pl.Buffered(1))


def _params(sem):
    return pltpu.CompilerParams(dimension_semantics=sem, vmem_limit_bytes=VMEM_LIMIT)


def _fft_tables_body(ka_ref, db_ref, perm_ref, *, n1, n2):
    half = n1 * FFT_ROWS
    i = lax.broadcasted_iota(jnp.int32, (2 * half, 2 * half), 0)
    j = lax.broadcasted_iota(jnp.int32, (2 * half, 2 * half), 1)
    k1 = (i & (half - 1)) // FFT_ROWS
    s1 = (j & (half - 1)) // FFT_ROWS
    same_r = (i & (FFT_ROWS - 1)) == (j & (FFT_ROWS - 1))
    ang = ((k1 * s1) & (n1 - 1)).astype(F32) * (2.0 * math.pi / n1)
    pi_, pj = i >= half, j >= half
    val = jnp.where(pi_ == pj, jnp.cos(ang), jnp.where(pj, -jnp.sin(ang), jnp.sin(ang)))
    ka_ref[...] = jnp.where(same_r, val, 0.0).astype(BF16)

    seq = n1 * n2
    k2 = lax.broadcasted_iota(jnp.int32, (n2, n2), 0)
    s2 = lax.broadcasted_iota(jnp.int32, (n2, n2), 1)
    for k1 in range(n1):
        ang = (((k1 + n1 * k2) * s2) & (seq - 1)).astype(F32) * (2.0 * math.pi / seq)
        db_ref[k1, :, :n2] = jnp.cos(ang).astype(BF16)
        db_ref[k1, :, n2:] = (-jnp.sin(ang)).astype(BF16)

    ts = perm_ref.shape[0]
    tok = lax.broadcasted_iota(jnp.int32, (ts, ts), 0)
    src = lax.broadcasted_iota(jnp.int32, (ts, ts), 1)
    perm_ref[...] = (src == (tok & (n1 - 1)) * (ts // n1) + tok // n1).astype(BF16)


def _fft_tables(seq):
    n1, n2 = FFT_N1, seq // FFT_N1
    half = n1 * FFT_ROWS
    ts = MIX_OUT_TILE // SUB_TILES
    return pl.pallas_call(
        functools.partial(_fft_tables_body, n1=n1, n2=n2),
        out_shape=[
            jax.ShapeDtypeStruct((2 * half, 2 * half), BF16),
            jax.ShapeDtypeStruct((n1, n2, 2 * n2), BF16),
            jax.ShapeDtypeStruct((ts, ts), BF16),
        ],
        compiler_params=pltpu.CompilerParams(vmem_limit_bytes=VMEM_LIMIT),
        name=f"fft_tables_{seq}",
    )()


def _prep_w_in_body(w_ref, o_ref, tab_ref):
    @pl.when((pl.program_id(0) == 0) & (pl.program_id(1) == 0))
    def _():
        r = lax.broadcasted_iota(jnp.int32, (D_FFT, 2 * D_FFT), 0)
        q = lax.broadcasted_iota(jnp.int32, (D_FFT, 2 * D_FFT), 1)
        same_group = (r // HEAD_DIM) == ((q & (D_FFT - 1)) // HEAD_DIM)
        cm = ((r & (HEAD_DIM - 1)) * (q & (HEAD_DIM - 1))) & (HEAD_DIM - 1)
        ang = cm.astype(F32) * (2.0 * math.pi / HEAD_DIM)
        tab_ref[...] = jnp.where(same_group, jnp.where(q < D_FFT, jnp.cos(ang), jnp.sin(ang)), 0.0)

    o_ref[0, :, :3 * D_CONV] = w_ref[0, :, :3 * D_CONV].astype(BF16)
    o_ref[0, :, 3 * D_CONV:] = jnp.dot(
        w_ref[0, :, 3 * D_CONV:], tab_ref[...], precision=lax.Precision.HIGHEST,
        preferred_element_type=F32).astype(BF16)


def _prep_w_in(w_in):
    depth = w_in.shape[0]
    rows = 256
    return pl.pallas_call(
        _prep_w_in_body,
        grid=(depth, D_MODEL // rows),
        in_specs=[pl.BlockSpec((1, rows, w_in.shape[-1]), lambda l, i: (l, i, 0))],
        out_specs=pl.BlockSpec((1, rows, IN_EXT), lambda l, i: (l, i, 0)),
        out_shape=jax.ShapeDtypeStruct((depth, D_MODEL, IN_EXT), BF16),
        scratch_shapes=[pltpu.VMEM((D_FFT, 2 * D_FFT), F32)],
        compiler_params=_params(("arbitrary", "arbitrary")),
        name="prep_w_in",
    )(w_in)


def _prep_w_out_body(w_ref, g_ref, o_ref):
    o_ref[0] = (w_ref[0] * g_ref[0]).astype(BF16)


def _prep_w_out(w_out, g_rows):
    depth, d, _ = w_out.shape
    return pl.pallas_call(
        _prep_w_out_body,
        grid=(depth,),
        in_specs=[pl.BlockSpec((1, d, d), lambda l: (l, 0, 0)),
                  pl.BlockSpec((1, d, 1), lambda l: (l, 0, 0))],
        out_specs=pl.BlockSpec((1, d, d), lambda l: (l, 0, 0)),
        out_shape=jax.ShapeDtypeStruct((depth, d, d), BF16),
        compiler_params=_params(("arbitrary",)),
        name="prep_w_out",
    )(w_out, g_rows.reshape(depth, d, 1))


def _prep_gate_up_body(g_ref, u_ref, o_ref):
    for j in range(g_ref.shape[-1] // MXU_COLS):
        src = slice(j * MXU_COLS, (j + 1) * MXU_COLS)
        o_ref[0, :, 2 * j * MXU_COLS:(2 * j + 1) * MXU_COLS] = g_ref[0, :, src].astype(BF16)
        o_ref[0, :, (2 * j + 1) * MXU_COLS:(2 * j + 2) * MXU_COLS] = u_ref[0, :, src].astype(BF16)


def _prep_gate_up(w_gate, w_up):
    depth, d, ff = w_gate.shape
    rows = 256
    src = pl.BlockSpec((1, rows, ff), lambda l, i: (l, i, 0))
    return pl.pallas_call(
        _prep_gate_up_body,
        grid=(depth, d // rows),
        in_specs=[src, src],
        out_specs=pl.BlockSpec((1, rows, 2 * ff), lambda l, i: (l, i, 0)),
        out_shape=jax.ShapeDtypeStruct((depth, d, 2 * ff), BF16),
        compiler_params=_params(("arbitrary", "arbitrary")),
        name="prep_gate_up",
    )(w_gate, w_up)


def _mod_body(c_ref, w_ref, b_ref, o_ref):
    c = c_ref[...]
    act = c * jax.nn.sigmoid(c)
    o_ref[0] = jnp.dot(act.astype(BF16), w_ref[0].astype(BF16),
                       preferred_element_type=F32) + b_ref[0]


def _modulation(c_all, w_ada, b_ada):
    depth = w_ada.shape[0]
    rows = c_all.shape[0]
    return pl.pallas_call(
        _mod_body,
        grid=(depth, N_MOD),
        in_specs=[
            pl.BlockSpec((rows, D_MODEL), lambda l, j: (0, 0)),
            pl.BlockSpec((1, D_MODEL, D_MODEL), lambda l, j: (l, 0, j)),
            pl.BlockSpec((1, 1, D_MODEL), lambda l, j: (l, 0, j)),
        ],
        out_specs=pl.BlockSpec((1, rows, D_MODEL), lambda l, j: (l, 0, j)),
        out_shape=jax.ShapeDtypeStruct((depth, rows, N_MOD * D_MODEL), F32),
        compiler_params=_params(("arbitrary", "arbitrary")),
        name="adaln_modulation",
    )(c_all, w_ada, b_ada.reshape(depth, 1, N_MOD * D_MODEL))


def _mix_in_body(x_ref, mod_ref, g_ref, w_ref, bg_ref, u_ref, a_ref):
    d = D_MODEL
    sh = mod_ref[:, 0 * d:1 * d]
    sc = mod_ref[:, 1 * d:2 * d]
    gain = g_ref[...] * (1.0 + sc)
    ts = x_ref.shape[1] // MIX_IN_SUB_TILES
    for sub in range(MIX_IN_SUB_TILES):
        rows = slice(sub * ts, (sub + 1) * ts)
        h = _rms(x_ref[0, rows, :], gain) + sh
        z = jnp.dot(h.astype(BF16), w_ref[...], preferred_element_type=F32)
        bg_ref[0, rows, :] = z[:, :D_CONV]
        u_ref[0, rows, :] = z[:, D_CONV:2 * D_CONV] * z[:, 2 * D_CONV:3 * D_CONV]
        a_ref[0, 0, rows, :] = z[:, 3 * D_CONV:3 * D_CONV + D_FFT].astype(BF16)
        a_ref[0, 1, rows, :] = z[:, 3 * D_CONV + D_FFT:].astype(BF16)


def _mix_in(x, mod_all, g_pre, w_in_ext, layer, row0):
    bsz, seq, d = x.shape
    tm = MIX_IN_TILE
    return pl.pallas_call(
        _mix_in_body,
        grid=(bsz, seq // tm),
        in_specs=[
            pl.BlockSpec((1, tm, d), lambda b, i: (b, i, 0)),
            pl.BlockSpec((None, None, 1, N_MOD * d), lambda b, i: (layer, row0 + b, 0, 0)),
            _resident((None, 1, d), lambda b, i: (layer, 0, 0)),
            _resident((None, d, IN_EXT), lambda b, i: (layer, 0, 0)),
        ],
        out_specs=[
            pl.BlockSpec((1, tm, D_CONV), lambda b, i: (b, i, 0)),
            pl.BlockSpec((1, tm, D_CONV), lambda b, i: (b, i, 0)),
            pl.BlockSpec((1, 2, tm, D_FFT), lambda b, i: (b, 0, i, 0)),
        ],
        out_shape=[
            jax.ShapeDtypeStruct((bsz, seq, D_CONV), F32),
            jax.ShapeDtypeStruct((bsz, seq, D_CONV), F32),
            jax.ShapeDtypeStruct((bsz, 2, seq, D_FFT), BF16),
        ],
        compiler_params=_params(("arbitrary", "arbitrary")),
        name="mix_in",
    )(x, mod_all, g_pre, w_in_ext)


def _seq_fft_body(a_ref, ka_ref, db_ref, o_ref, y_ref, *, n1, n2):
    rows = FFT_ROWS
    half = n1 * rows
    for j in range(n2 // rows):
        r0 = j * rows
        x = jnp.concatenate([a_ref[0, 0, :, r0:r0 + rows, :].reshape(half, D_FFT),
                             a_ref[0, 1, :, r0:r0 + rows, :].reshape(half, D_FFT)], axis=0)
        y = jnp.dot(ka_ref[...], x, preferred_element_type=F32)
        y_ref[:, 0, r0:r0 + rows, :] = y[:half].astype(BF16).reshape(n1, rows, D_FFT)
        y_ref[:, 1, r0:r0 + rows, :] = y[half:].astype(BF16).reshape(n1, rows, D_FFT)

    for k1 in range(n1):
        f = jnp.dot(db_ref[k1], y_ref[k1].reshape(2 * n2, D_FFT), preferred_element_type=F32)
        o_ref[0, k1] = _rms(f).astype(BF16)


def _seq_fft(tables, a):
    ka, db, _ = tables
    bsz, _, seq, _ = a.shape
    n1, n2 = FFT_N1, seq // FFT_N1
    a5 = a.reshape(bsz, 2, n1, n2, D_FFT)
    return pl.pallas_call(
        functools.partial(_seq_fft_body, n1=n1, n2=n2),
        grid=(bsz,),
        in_specs=[
            pl.BlockSpec((1, 2, n1, n2, D_FFT), lambda b: (b, 0, 0, 0, 0)),
            _resident(ka.shape, lambda b: (0, 0)),
            _resident(db.shape, lambda b: (0, 0, 0)),
        ],
        out_specs=pl.BlockSpec((1, n1, n2, D_FFT), lambda b: (b, 0, 0, 0)),
        out_shape=jax.ShapeDtypeStruct((bsz, n1, n2, D_FFT), BF16),
        scratch_shapes=[pltpu.VMEM((n1, 2, n2, D_FFT), BF16)],
        compiler_params=_params(("arbitrary",)),
        name="seq_fft",
    )(a5, ka, db)


def _mix_out_body(x_ref, bg_ref, u_ref, up_ref, un_ref, fn_ref, perm_ref, mod_ref, cw_ref, cb_ref,
                  gpm_ref, wo_ref, gpf_ref, gpo_ref, wgu_ref, wd_ref, o_ref):
    d = D_MODEL
    tm = u_ref.shape[1]
    ts = tm // SUB_TILES
    i = pl.program_id(1)
    last = pl.num_programs(1) - 1

    mix_gain = mod_ref[:, 2 * d:3 * d] * gpm_ref[...]
    ffn_in_gain = gpf_ref[...] * (1.0 + mod_ref[:, 4 * d:5 * d])
    ffn_out_gain = mod_ref[:, 5 * d:6 * d] * gpo_ref[...]

    def rows_of(sub):
        return slice(sub * ts, (sub + 1) * ts)

    def conv_branch(sub):
        r0 = sub * ts
        u = u_ref[0, rows_of(sub), :].astype(F32)
        if sub == 0:
            prev = jnp.where(i > 0, up_ref[0, CONV_HALO - 1:CONV_HALO, :].astype(F32), 0.0)
        else:
            prev = u_ref[0, r0 - 1:r0, :].astype(F32)
        if sub == SUB_TILES - 1:
            nxt = jnp.where(i < last, un_ref[0, 0:1, :].astype(F32), 0.0)
        else:
            nxt = u_ref[0, r0 + ts:r0 + ts + 1, :].astype(F32)
        row = lax.broadcasted_iota(jnp.int32, u.shape, 0)
        u_m1 = jnp.where(row == 0, prev, pltpu.roll(u, 1, 0))
        u_p1 = jnp.where(row == ts - 1, nxt, pltpu.roll(u, ts - 1, 0))
        conv = u_m1 * cw_ref[0:1, :] + u * cw_ref[1:2, :] + u_p1 * cw_ref[2:3, :] + cb_ref[...]
        conv_n = _rms(bg_ref[0, rows_of(sub), :] * conv).astype(BF16)
        q = ts // FFT_N1
        fn = fn_ref[0, :, sub * q:(sub + 1) * q, :].reshape(ts, D_FFT)
        fn = jnp.dot(perm_ref[...], fn, preferred_element_type=F32).astype(BF16)
        return jnp.concatenate([conv_n, fn], axis=-1)

    def out_proj(merged):
        return jnp.dot(merged, wo_ref[...], preferred_element_type=F32)

    def residual_and_ffn_input(sub, o):
        x1 = x_ref[0, rows_of(sub), :] + _rms(o, mix_gain)
        h = _rms(x1, ffn_in_gain) + mod_ref[:, 3 * d:4 * d]
        return x1, h.astype(BF16)

    def ffn_chunk(hb, ff, c0):
        c1 = min(c0 + FF_CHUNK, D_FF)
        gu = jnp.dot(hb, wgu_ref[:, 2 * c0:2 * c1], preferred_element_type=F32)
        acts = []
        for b0 in range(0, 2 * (c1 - c0), 2 * MXU_COLS):
            gate = gu[:, b0:b0 + MXU_COLS]
            up = gu[:, b0 + MXU_COLS:b0 + 2 * MXU_COLS]
            half = 0.5 * gate
            acts.append((half * (1.0 + jnp.tanh(half)) * up).astype(BF16))
        act = jnp.concatenate(acts, axis=-1)
        part = jnp.dot(act, wd_ref[c0:c1, :], preferred_element_type=F32)
        return part if ff is None else ff + part

    def finish(sub, x1, ff):
        o_ref[0, rows_of(sub), :] = x1 + _rms(ff, ffn_out_gain)

    chunks = list(range(0, D_FF, FF_CHUNK))
    x1, hb = residual_and_ffn_input(0, out_proj(conv_branch(0)))
    pending = None
    for sub in range(SUB_TILES):
        more = sub + 1 < SUB_TILES
        if more:
            o_next = out_proj(conv_branch(sub + 1))
        ff = ffn_chunk(hb, None, chunks[0])
        if pending is not None:
            finish(*pending)
        if more:
            x1_next, hb_next = residual_and_ffn_input(sub + 1, o_next)
        for c0 in chunks[1:]:
            ff = ffn_chunk(hb, ff, c0)
        pending = (sub, x1, ff)
        if more:
            x1, hb = x1_next, hb_next
    finish(*pending)


def _mix_out(x, bg, u, fn, perm, mod_all, conv_w, conv_b, g_post_mix, w_out, g_pre_ffn,
             g_post_ffn, w_gate_up, w_down, layer, row0):
    bsz, seq, d = x.shape
    tm = MIX_OUT_TILE
    halo_per_tile = tm // CONV_HALO
    n_halo = seq // CONV_HALO
    tok = lambda b, i: (b, i, 0)
    lay = lambda b, i: (layer, 0, 0)
    return pl.pallas_call(
        _mix_out_body,
        grid=(bsz, seq // tm),
        in_specs=[
            pl.BlockSpec((1, tm, d), tok),
            pl.BlockSpec((1, tm, D_CONV), tok),
            pl.BlockSpec((1, tm, D_CONV), tok),
            pl.BlockSpec((1, CONV_HALO, D_CONV),
                         lambda b, i: (b, jnp.maximum(i * halo_per_tile - 1, 0), 0)),
            pl.BlockSpec((1, CONV_HALO, D_CONV),
                         lambda b, i: (b, jnp.minimum((i + 1) * halo_per_tile, n_halo - 1), 0)),
            pl.BlockSpec((1, FFT_N1, tm // FFT_N1, D_FFT), lambda b, i: (b, 0, i, 0)),
            _resident(perm.shape, lambda b, i: (0, 0)),
            pl.BlockSpec((None, None, 1, N_MOD * d), lambda b, i: (layer, row0 + b, 0, 0)),
            _resident((None, 3, D_CONV), lay),
            _resident((None, 1, D_CONV), lay),
            _resident((None, 1, d), lay),
            _resident((None, d, d), lay),
            _resident((None, 1, d), lay),
            _resident((None, 1, d), lay),
            _resident((None, d, 2 * D_FF), lay),
            _resident((None, D_FF, d), lay),
        ],
        out_specs=pl.BlockSpec((1, tm, d), tok),
        out_shape=jax.ShapeDtypeStruct((bsz, seq, d), F32),
        compiler_params=_params(("arbitrary", "arbitrary")),
        name="mix_out",
    )(x, bg, u, u, u, fn, perm, mod_all, conv_w, conv_b, g_post_mix, w_out, g_pre_ffn,
      g_post_ffn, w_gate_up, w_down)


def kernel(x_prompt, x_sample, c_prompt, c_sample, w_ada, b_ada, g_pre_mix, g_post_mix, w_in,
           conv_w, conv_b, g_conv, g_fft, w_out, g_pre_ffn, g_post_ffn, w_gate, w_up, w_down):
    depth = w_in.shape[0]
    row3 = lambda g: g.reshape(depth, 1, g.shape[-1])

    w_in_ext = _prep_w_in(w_in)
    w_out_b = _prep_w_out(w_out, jnp.concatenate([g_conv, g_fft], axis=-1))
    w_gate_up = _prep_gate_up(w_gate, w_up)
    w_down_b = w_down.astype(BF16)

    c_all = jnp.concatenate([c_prompt, c_sample], axis=0)
    mod_all = _modulation(c_all, w_ada, b_ada)
    mod_all = mod_all.reshape(depth, c_all.shape[0], 1, N_MOD * D_MODEL)

    g_pre_mix3, g_post_mix3 = row3(g_pre_mix), row3(g_post_mix)
    conv_b3 = row3(conv_b)
    g_pre_ffn3, g_post_ffn3 = row3(g_pre_ffn), row3(g_post_ffn)

    outs = []
    for x, row0 in ((x_prompt, 0), (x_sample, c_prompt.shape[0])):
        tables = _fft_tables(x.shape[1])
        for layer in range(depth):
            bg, u, a = _mix_in(x, mod_all, g_pre_mix3, w_in_ext, layer, row0)
            fn = _seq_fft(tables, a)
            x = _mix_out(x, bg, u, fn, tables[2], mod_all, conv_w, conv_b3, g_post_mix3, w_out_b,
                         g_pre_ffn3, g_post_ffn3, w_gate_up, w_down_b, layer, row0)
        outs.append(x)
    return tuple(outs)
```
